```python
import math
import functools
import jax
import jax.numpy as jnp
from jax import lax
import numpy as np

D_MODEL = 2048
BATCH = 4
SEQ = 4096
DEPTH = 2

GRID_W = 64
CTX_LEN = 256
CHUNK = 64
CONV_K = 3
N_MOD = 9
N_BRANCH = 3
D_FF = 11 * D_MODEL // 4
SSD_HEAD_DIM = 64
SSD_INNER = D_MODEL // 2
SSD_HEADS = SSD_INNER // SSD_HEAD_DIM
SSD_GROUPS = 4
SSD_HPG = SSD_HEADS // SSD_GROUPS
SSD_STATE = 64
SSD_XBC = SSD_INNER + 2 * SSD_GROUPS * SSD_STATE
ML_WIDTH = D_MODEL // 4
ML_HEADS = 4
ML_HEAD_DIM = ML_WIDTH // ML_HEADS
GLA_WIDTH = D_MODEL // 4
GLA_HEADS = 4
GLA_DV = GLA_WIDTH // GLA_HEADS
GLA_KEY_WIDTH = GLA_WIDTH // 2
GLA_DK = GLA_KEY_WIDTH // GLA_HEADS
GLA_RANK = 16
GLA_TAU = 16.0
DEEPNORM_ALPHA = (2 * DEPTH) ** 0.25
DEEPNORM_BETA = (8 * DEPTH) ** -0.25
EPS = 1e-5
IN_SIZES = (SSD_INNER, SSD_XBC, 2 * SSD_HEADS,
            ML_WIDTH, ML_WIDTH, ML_WIDTH, ML_WIDTH, 4 * ML_HEADS,
            GLA_KEY_WIDTH, GLA_KEY_WIDTH, GLA_WIDTH, GLA_WIDTH, 2 * GLA_RANK,
            N_BRANCH * D_MODEL)
IN_TOTAL = sum(IN_SIZES)

kernel_name = 'hybrid_ssd_mlstm_gla_diffusion_block'


def _split(t, sizes):
    idx = [int(s) for s in np.cumsum(sizes)[:-1]]
    return jnp.split(t, idx, axis=-1)


def _chunks(t):
    return t.reshape(t.shape[0], t.shape[1] // CHUNK, CHUNK, *t.shape[2:])


def _causal_mask():
    return jnp.tril(jnp.ones((CHUNK, CHUNK), dtype=bool))


def layer_norm(x, g, b):
    xf = x.astype(jnp.float32)
    mu = jnp.mean(xf, axis=-1, keepdims=True)
    var = jnp.mean(jnp.square(xf - mu), axis=-1, keepdims=True)
    return ((xf - mu) * lax.rsqrt(var + EPS)).astype(x.dtype) * g + b


def group_norm(y, w, groups, center):
    shp = y.shape
    yf = y.astype(jnp.float32).reshape(*shp[:-1], groups, shp[-1] // groups)
    if center:
        yf = yf - jnp.mean(yf, axis=-1, keepdims=True)
    yf = yf * lax.rsqrt(jnp.mean(jnp.square(yf), axis=-1, keepdims=True) + EPS)
    return yf.reshape(shp).astype(y.dtype) * w


def swiglu(u, w_in, w_out):
    a, g = jnp.split(u @ w_in, 2, axis=-1)
    return (jax.nn.silu(g) * a) @ w_out


def short_conv(u, uc, w, b):
    bsz, T, ch = u.shape
    rows = T // GRID_W
    grid = u.reshape(bsz, rows, GRID_W, ch)
    y = lax.conv_general_dilated(grid, w[:, :, None, :], window_strides=(1, 1), padding='SAME',
                                 dimension_numbers=('NHWC', 'HWIO', 'NHWC'), feature_group_count=ch)
    yc = lax.conv_general_dilated(uc, w[1][:, None, :], window_strides=(1,), padding='SAME',
                                  dimension_numbers=('NWC', 'WIO', 'NWC'), feature_group_count=ch)
    return y.reshape(bsz, T, ch) + b, yc + b


def ssd_chunked(x, dt, bm, cm, state, need_out, a):
    out_dtype = x.dtype
    bsz, T = x.shape[:2]
    x, dt, bm, cm = (_chunks(t.astype(jnp.float32)) for t in (x, dt, bm, cm))
    acum = jnp.cumsum(dt * a, axis=2)
    a_end = acum[:, :, -1]
    w_end = jnp.exp(a_end[:, :, None] - acum) * dt
    s_loc = jnp.einsum('bclgn,bclgh,bclghp->bcghpn', bm, w_end, x)

    def step(h, inp):
        dec, s = inp
        return h * jnp.exp(dec)[..., None, None] + s, h

    h_fin, h_in = lax.scan(step, state, (jnp.moveaxis(a_end, 1, 0), jnp.moveaxis(s_loc, 1, 0)))
    if not need_out:
        return None, h_fin
    h_in = jnp.moveaxis(h_in, 0, 1)
    seg = acum[:, :, :, None] - acum[:, :, None, :]
    decay = jnp.exp(jnp.where(_causal_mask()[:, :, None, None], seg, -jnp.inf))
    cb = jnp.einsum('bclgn,bcsgn->bclsg', cm, bm)
    y = jnp.einsum('bclsg,bclsgh,bcsgh,bcsghp->bclghp', cb, decay, dt, x)
    y = y + jnp.einsum('bclgn,bcghpn,bclgh->bclghp', cm, h_in, jnp.exp(acum))
    return y.reshape(bsz, T, *y.shape[3:]).astype(out_dtype), h_fin


def mlstm_chunked(q, k, v, i_pre, f_pre, state, need_out):
    out_dtype = v.dtype
    bsz, T = v.shape[:2]
    q, k, v, ig, fg = (_chunks(t.astype(jnp.float32)) for t in (q, k, v, i_pre, f_pre))
    q = q * (ML_HEAD_DIM ** -0.5)
    b = jnp.cumsum(jax.nn.log_sigmoid(fg), axis=2)
    b_end = b[:, :, -1]
    a = b_end[:, :, None] - b + ig
    m_loc = jnp.max(a, axis=2)
    w = jnp.exp(a - m_loc[:, :, None])
    c_loc = jnp.einsum('bcjh,bcjhv,bcjhk->bchvk', w, v, k)
    n_loc = jnp.einsum('bcjh,bcjhk->bchk', w, k)

    def step(carry, inp):
        c_st, n_st, m_st = carry
        be, ml, cl, nl = inp
        m_new = jnp.maximum(be + m_st, ml)
        f_old = jnp.exp(be + m_st - m_new)
        f_new = jnp.exp(ml - m_new)
        new = (f_old[..., None, None] * c_st + f_new[..., None, None] * cl,
               f_old[..., None] * n_st + f_new[..., None] * nl,
               m_new)
        return new, carry

    fin, ent = lax.scan(step, state, tuple(jnp.moveaxis(t, 1, 0) for t in (b_end, m_loc, c_loc, n_loc)))
    if not need_out:
        return None, fin
    c_in, n_in, m_in = (jnp.moveaxis(t, 0, 1) for t in ent)
    dmat = b[:, :, :, None, :] - b[:, :, None, :, :] + ig[:, :, None, :, :]
    dmat = jnp.where(_causal_mask()[:, :, None], dmat, -jnp.inf)
    g = b + m_in[:, :, None, :]
    m_t = jnp.maximum(jnp.max(dmat, axis=3), g)
    s = jnp.exp(dmat - m_t[:, :, :, None, :]) * jnp.einsum('bcthk,bcjhk->bctjh', q, k)
    g_w = jnp.exp(g - m_t)
    num = jnp.einsum('bctjh,bcjhv->bcthv', s, v) + g_w[..., None] * jnp.einsum('bcthk,bchvk->bcthv', q, c_in)
    den = jnp.sum(s, axis=3) + g_w * jnp.einsum('bcthk,bchk->bcth', q, n_in)
    h = num / jnp.maximum(jnp.abs(den), jnp.exp(-m_t))[..., None]
    return h.reshape(bsz, T, *h.shape[3:]).astype(out_dtype), fin


def gla_chunked(q, k, v, log_a, state, need_out):
    out_dtype = v.dtype
    bsz, T = v.shape[:2]
    q, k, v, log_a = (_chunks(t.astype(jnp.float32)) for t in (q, k, v, log_a))
    q = q * (GLA_DK ** -0.5)
    b = jnp.cumsum(log_a, axis=2)
    b_end = b[:, :, -1]
    s_loc = jnp.einsum('bcjhk,bcjhv->bchkv', k * jnp.exp(b_end[:, :, None] - b), v)

    def step(s_st, inp):
        be, sl = inp
        return jnp.exp(be)[..., None] * s_st + sl, s_st

    s_fin, s_in = lax.scan(step, state, (jnp.moveaxis(b_end, 1, 0), jnp.moveaxis(s_loc, 1, 0)))
    if not need_out:
        return None, s_fin
    s_in = jnp.moveaxis(s_in, 0, 1)
    q_d = q * jnp.exp(b)
    k_d = k * jnp.exp(-b)
    att = jnp.where(_causal_mask(), jnp.einsum('bcthk,bcjhk->bchtj', q_d, k_d), 0.0)
    o = jnp.einsum('bchtj,bcjhv->bcthv', att, v) + jnp.einsum('bcthk,bchkv->bcthv', q_d, s_in)
    return o.reshape(bsz, T, *o.shape[3:]).astype(out_dtype), s_fin


def bidirectional(fn_f, fn_b, ctx_f, lat_f, ctx_b, lat_b, state0, need_ctx_out):
    rev = lambda ts: tuple(jnp.flip(t, axis=1) for t in ts)
    yc_f, s_f = fn_f(*ctx_f, state0, need_ctx_out)
    yl_f, _ = fn_f(*lat_f, s_f, True)
    yc_b, s_b = fn_b(*rev(ctx_b), state0, need_ctx_out)
    yl_b, _ = fn_b(*rev(lat_b), s_b, True)
    y_lat = yl_f + jnp.flip(yl_b, axis=1)
    y_ctx = (yc_f + jnp.flip(yc_b, axis=1)) if need_ctx_out else None
    return y_lat, y_ctx


def ssd_branch(lat, ctx, need_ctx_out, conv_w, conv_b, dt_bias, a_log, d_skip, norm_w, w_br):
    (z, xbc, dt), (zc, xbcc, dtc) = lat, ctx
    xbc, xbcc = short_conv(xbc, xbcc, conv_w, conv_b)

    def prep(t_xbc, t_dt):
        bsz, T = t_xbc.shape[:2]
        xs, bm, cm = _split(jax.nn.silu(t_xbc), (SSD_INNER, SSD_GROUPS * SSD_STATE, SSD_GROUPS * SSD_STATE))
        dts = jax.nn.softplus(t_dt.reshape(bsz, T, 2, SSD_GROUPS, SSD_HPG).astype(jnp.float32)
                              + dt_bias.reshape(2, SSD_GROUPS, SSD_HPG).astype(jnp.float32))
        return (xs.reshape(bsz, T, SSD_GROUPS, SSD_HPG, SSD_HEAD_DIM),
                bm.reshape(bsz, T, SSD_GROUPS, SSD_STATE),
                cm.reshape(bsz, T, SSD_GROUPS, SSD_STATE),
                dts[:, :, 0], dts[:, :, 1])

    xs, bm, cm, dt_f, dt_b = prep(xbc, dt)
    xsc, bmc, cmc, dtc_f, dtc_b = prep(xbcc, dtc)
    a = -jnp.exp(a_log.astype(jnp.float32)).reshape(2, SSD_GROUPS, SSD_HPG)
    state0 = jnp.zeros((xs.shape[0], SSD_GROUPS, SSD_HPG, SSD_HEAD_DIM, SSD_STATE), jnp.float32)
    y, yc = bidirectional(functools.partial(ssd_chunked, a=a[0]), functools.partial(ssd_chunked, a=a[1]),
                          (xsc, dtc_f, bmc, cmc), (xs, dt_f, bm, cm),
                          (xsc, dtc_b, bmc, cmc), (xs, dt_b, bm, cm), state0, need_ctx_out)

    def finish(yy, xx, zz):
        yy = (yy + d_skip.reshape(SSD_GROUPS, SSD_HPG, 1) * xx).reshape(zz.shape)
        return group_norm(yy * jax.nn.silu(zz), norm_w, SSD_GROUPS, False) @ w_br

    return finish(y, xs, z), (finish(yc, xsc, zc) if need_ctx_out else None)


def mlstm_branch(lat, ctx, need_ctx_out, conv_w, conv_b, gate_b, norm_w, w_br):
    (q, k, v, o, g), (qc, kc, vc, oc, gc) = lat, ctx
    qk, qkc = short_conv(jnp.concatenate([q, k], axis=-1), jnp.concatenate([qc, kc], axis=-1), conv_w, conv_b)

    def prep(t_qk, t_v, t_g):
        bsz, T = t_v.shape[:2]
        hq, hk = jnp.split(jax.nn.silu(t_qk), 2, axis=-1)
        hd = (bsz, T, ML_HEADS, ML_HEAD_DIM)
        gates = t_g.reshape(bsz, T, 2, 2, ML_HEADS) + gate_b
        return hq.reshape(hd), hk.reshape(hd), t_v.reshape(hd), gates

    hq, hk, hv, gt = prep(qk, v, g)
    hqc, hkc, hvc, gtc = prep(qkc, vc, gc)
    bsz = hq.shape[0]
    state0 = (jnp.zeros((bsz, ML_HEADS, ML_HEAD_DIM, ML_HEAD_DIM), jnp.float32),
              jnp.zeros((bsz, ML_HEADS, ML_HEAD_DIM), jnp.float32),
              jnp.zeros((bsz, ML_HEADS), jnp.float32))
    y, yc = bidirectional(mlstm_chunked, mlstm_chunked,
                          (hqc, hkc, hvc, gtc[:, :, 0, 0], gtc[:, :, 0, 1]),
                          (hq, hk, hv, gt[:, :, 0, 0], gt[:, :, 0, 1]),
                          (hqc, hkc, hvc, gtc[:, :, 1, 0], gtc[:, :, 1, 1]),
                          (hq, hk, hv, gt[:, :, 1, 0], gt[:, :, 1, 1]), state0, need_ctx_out)

    def finish(hh, oo):
        hh = jax.nn.sigmoid(oo) * hh.reshape(oo.shape)
        return group_norm(hh, norm_w, ML_HEADS, True) @ w_br

    return finish(y, o), (finish(yc, oc) if need_ctx_out else None)


def gla_branch(lat, ctx, need_ctx_out, w2, b2, norm_w, w_br):
    (q, k, v, r, lr), (qc, kc, vc, rc, lrc) = lat, ctx

    def prep(tq, tk, tv, tlr):
        bsz, T = tq.shape[:2]
        kd = (bsz, T, GLA_HEADS, GLA_DK)
        lr2 = tlr.reshape(bsz, T, 2, GLA_RANK)
        log_a = [(jax.nn.log_sigmoid((lr2[:, :, d] @ w2[d] + b2[d]).astype(jnp.float32)) / GLA_TAU).reshape(kd)
                 for d in range(2)]
        return tq.reshape(kd), tk.reshape(kd), tv.reshape(bsz, T, GLA_HEADS, GLA_DV), log_a[0], log_a[1]

    gq, gk, gv, la_f, la_b = prep(q, k, v, lr)
    gqc, gkc, gvc, lac_f, lac_b = prep(qc, kc, vc, lrc)
    state0 = jnp.zeros((gq.shape[0], GLA_HEADS, GLA_DK, GLA_DV), jnp.float32)
    y, yc = bidirectional(gla_chunked, gla_chunked,
                          (gqc, gkc, gvc, lac_f), (gq, gk, gv, la_f),
                          (gqc, gkc, gvc, lac_b), (gq, gk, gv, la_b), state0, need_ctx_out)

    def finish(oo, rr):
        return (group_norm(oo.reshape(rr.shape), norm_w, GLA_HEADS, True) * jax.nn.silu(rr)) @ w_br

    return finish(y, r), (finish(yc, rc) if need_ctx_out else None)


def _merge(b_ssd, b_ml, b_gla, gate_pre, merge_b, w_out):
    g = jax.nn.sigmoid(gate_pre.reshape(*gate_pre.shape[:-1], N_BRANCH, D_MODEL) + merge_b)
    return (g[..., 0, :] * b_ssd + g[..., 1, :] * b_ml + g[..., 2, :] * b_gla) @ w_out


def token_mixer(u, uc, need_ctx_out, w_in, merge_b, ssd_conv_w, ssd_conv_b, ssd_dt_bias, ssd_a_log, ssd_d,
                ssd_norm_w, ml_conv_w, ml_conv_b, ml_gate_b, ml_norm_w, gla_w2, gla_b2, gla_norm_w,
                w_br_ssd, w_br_ml, w_br_gla, w_out):
    pl = _split(u @ w_in, IN_SIZES)
    pc = _split(uc @ w_in, IN_SIZES)
    s_lat, s_ctx = ssd_branch(pl[0:3], pc[0:3], need_ctx_out, ssd_conv_w, ssd_conv_b, ssd_dt_bias,
                              ssd_a_log, ssd_d, ssd_norm_w, w_br_ssd)
    m_lat, m_ctx = mlstm_branch(pl[3:8], pc[3:8], need_ctx_out, ml_conv_w, ml_conv_b, ml_gate_b,
                                ml_norm_w, w_br_ml)
    g_lat, g_ctx = gla_branch(pl[8:13], pc[8:13], need_ctx_out, gla_w2, gla_b2, gla_norm_w, w_br_gla)
    y = _merge(s_lat, m_lat, g_lat, pl[13], merge_b, w_out)
    yc = _merge(s_ctx, m_ctx, g_ctx, pc[13], merge_b, w_out) if need_ctx_out else None
    return y, yc


def _sub_in(h, m, k):
    return h * (1.0 + m[3 * k + 1]) + m[3 * k]


def _sub_out(h, y, m, k, g, b):
    return layer_norm(DEEPNORM_ALPHA * h + m[3 * k + 2] * y, g, b)


def setup_inputs(seed: int = 0) -> dict:
    key = jax.random.key(seed)
    ks = list(jax.random.split(key, 32))

    def nrm(i, shape, scale):
        return jax.random.normal(ks[i], shape, jnp.float32) * scale

    D, L = D_MODEL, DEPTH
    u_dt = jax.random.uniform(ks[30], (L, 2, SSD_HEADS), jnp.float32)
    dt0 = jnp.exp(u_dt * (math.log(0.1) - math.log(0.001)) + math.log(0.001))
    f_bias = jnp.stack([jnp.zeros((ML_HEADS,), jnp.float32), jnp.linspace(3.0, 6.0, ML_HEADS, dtype=jnp.float32)])
    return {
        'x': nrm(0, (BATCH, SEQ, D), 1.0),
        'c': nrm(1, (BATCH, D), 1.0),
        'ctx': nrm(2, (BATCH, CTX_LEN, D), 1.0),
        'c_ctx': nrm(3, (D,), 1.0),
        'w_mod': nrm(4, (L, D, N_MOD * D), 0.5 * D ** -0.5),
        'b_mod': nrm(5, (L, N_MOD * D), 0.02),
        'ln_g': 1.0 + nrm(6, (L, 3, D), 0.02),
        'ln_b': nrm(7, (L, 3, D), 0.02),
        'ffn_w_in': nrm(8, (L, 2, D, 2 * D_FF), D ** -0.5),
        'ffn_w_out': nrm(9, (L, 2, D_FF, D), DEEPNORM_BETA * D_FF ** -0.5),
        'w_in': nrm(10, (L, D, IN_TOTAL), D ** -0.5),
        'merge_b': nrm(11, (L, N_BRANCH, D), 0.02),
        'ssd_conv_w': nrm(12, (L, CONV_K, CONV_K, SSD_XBC), 1.0 / CONV_K),
        'ssd_conv_b': nrm(13, (L, SSD_XBC), 0.02),
        'ssd_dt_bias': dt0 + jnp.log(-jnp.expm1(-dt0)),
        'ssd_a_log': jnp.log(jax.random.uniform(ks[14], (L, 2, SSD_HEADS), jnp.float32, 1.0, 16.0)),
        'ssd_d': 1.0 + nrm(15, (L, SSD_HEADS), 0.1),
        'ssd_norm_w': 1.0 + nrm(16, (L, SSD_INNER), 0.02),
        'ml_conv_w': nrm(17, (L, CONV_K, CONV_K, 2 * ML_WIDTH), 1.0 / CONV_K),
        'ml_conv_b': nrm(18, (L, 2 * ML_WIDTH), 0.02),
        'ml_gate_b': nrm(19, (L, 2, 2, ML_HEADS), 0.1) + f_bias,
        'ml_norm_w': 1.0 + nrm(20, (L, ML_WIDTH), 0.02),
        'gla_w2': nrm(21, (L, 2, GLA_RANK, GLA_KEY_WIDTH), GLA_RANK ** -0.5),
        'gla_b2': nrm(22, (L, 2, GLA_KEY_WIDTH), 0.1),
        'gla_norm_w': 1.0 + nrm(23, (L, GLA_WIDTH), 0.02),
        'w_br_ssd': nrm(24, (L, SSD_INNER, D), DEEPNORM_BETA * SSD_INNER ** -0.5),
        'w_br_ml': nrm(25, (L, ML_WIDTH, D), DEEPNORM_BETA * ML_WIDTH ** -0.5),
        'w_br_gla': nrm(26, (L, GLA_WIDTH, D), DEEPNORM_BETA * GLA_WIDTH ** -0.5),
        'w_out': nrm(27, (L, D, D), DEEPNORM_BETA * D ** -0.5),
    }


def reference(x, c, ctx, c_ctx, w_mod, b_mod, ln_g, ln_b, ffn_w_in, ffn_w_out, w_in, merge_b,
              ssd_conv_w, ssd_conv_b, ssd_dt_bias, ssd_a_log, ssd_d, ssd_norm_w,
              ml_conv_w, ml_conv_b, ml_gate_b, ml_norm_w, gla_w2, gla_b2, gla_norm_w,
              w_br_ssd, w_br_ml, w_br_gla, w_out):
    h, hc = x, ctx
    for l in range(DEPTH):
        need_ctx_out = l < DEPTH - 1
        m_lat = jnp.moveaxis((jax.nn.silu(c) @ w_mod[l] + b_mod[l]).reshape(-1, N_MOD, D_MODEL), 1, 0)[:, :, None, :]
        m_ctx = (jax.nn.silu(c_ctx) @ w_mod[l] + b_mod[l]).reshape(N_MOD, D_MODEL)
        h = _sub_out(h, 0.5 * swiglu(_sub_in(h, m_lat, 0), ffn_w_in[l, 0], ffn_w_out[l, 0]), m_lat, 0, ln_g[l, 0], ln_b[l, 0])
        hc = _sub_out(hc, 0.5 * swiglu(_sub_in(hc, m_ctx, 0), ffn_w_in[l, 0], ffn_w_out[l, 0]), m_ctx, 0, ln_g[l, 0], ln_b[l, 0])
        y, yc = token_mixer(_sub_in(h, m_lat, 1), _sub_in(hc, m_ctx, 1), need_ctx_out, w_in[l], merge_b[l],
                            ssd_conv_w[l], ssd_conv_b[l], ssd_dt_bias[l], ssd_a_log[l], ssd_d[l], ssd_norm_w[l],
                            ml_conv_w[l], ml_conv_b[l], ml_gate_b[l], ml_norm_w[l],
                            gla_w2[l], gla_b2[l], gla_norm_w[l], w_br_ssd[l], w_br_ml[l], w_br_gla[l], w_out[l])
        h = _sub_out(h, y, m_lat, 1, ln_g[l, 1], ln_b[l, 1])
        h = _sub_out(h, 0.5 * swiglu(_sub_in(h, m_lat, 2), ffn_w_in[l, 1], ffn_w_out[l, 1]), m_lat, 2, ln_g[l, 2], ln_b[l, 2])
        if need_ctx_out:
            hc = _sub_out(hc, yc, m_ctx, 1, ln_g[l, 1], ln_b[l, 1])
            hc = _sub_out(hc, 0.5 * swiglu(_sub_in(hc, m_ctx, 2), ffn_w_in[l, 1], ffn_w_out[l, 1]), m_ctx, 2, ln_g[l, 2], ln_b[l, 2])
    return h
```

```python
import functools

import jax
import jax.numpy as jnp
from jax import lax
from jax.experimental import pallas as pl
from jax.experimental.pallas import tpu as pltpu

F32 = jnp.float32
BF16 = jnp.bfloat16
HIGHEST = lax.Precision.HIGHEST

GRID_W = 64
CHUNK = 64
SSD_GROUPS = 4
SSD_STATE = 64
GLA_HEADS = 4
GLA_TAU = 16.0
EPS = 1e-5
N_MOD = 9

SCAN_BLOCK = 256
SMALL_W = 128
V7X_VMEM_LIMIT = 56 * 1024 * 1024

NT_DIMS = (((1,), (1,)), ((), ()))
TN_DIMS = (((0,), (0,)), ((), ()))


def _cparams(sem, vmem=V7X_VMEM_LIMIT):
    return pltpu.CompilerParams(dimension_semantics=sem, vmem_limit_bytes=vmem)


def _dot(a, b):
    return jnp.dot(a, b, preferred_element_type=F32)


def _dot_exact(a, b):
    return jnp.dot(a, b, preferred_element_type=F32, precision=HIGHEST)


def _dot_nt(a, b):
    return lax.dot_general(a, b, NT_DIMS, preferred_element_type=F32)


def _dot_tn(a, b):
    return lax.dot_general(a, b, TN_DIMS, preferred_element_type=F32)


def _sigmoid(x):
    return 1.0 / (1.0 + jnp.exp(-x))


def _silu(x):
    return x * _sigmoid(x)


def _softplus(x):
    return jnp.maximum(x, 0.0) + jnp.log(1.0 + jnp.exp(-jnp.abs(x)))


def _log_sigmoid(x):
    return -_softplus(-x)


def _layer_norm(x, g, b):
    mu = jnp.mean(x, axis=-1, keepdims=True)
    xc = x - mu
    var = jnp.mean(xc * xc, axis=-1, keepdims=True)
    return xc * lax.rsqrt(var + EPS) * g + b


def _mod_kernel(c_ref, w_ref, b_ref, o_ref):
    c = c_ref[...]
    o_ref[0] = _dot_exact(_silu(c), w_ref[0]) + b_ref[0]


def _modulation(cpad, w_mod, b_mod, tn=1152):
    depth, d, n = w_mod.shape
    rows = cpad.shape[0]
    return pl.pallas_call(
        _mod_kernel,
        grid=(depth, n // tn),
        in_specs=[pl.BlockSpec((rows, d), lambda l, j: (0, 0)),
                  pl.BlockSpec((1, d, tn), lambda l, j: (l, 0, j)),
                  pl.BlockSpec((1, 1, tn), lambda l, j: (l, 0, j))],
        out_specs=pl.BlockSpec((1, rows, tn), lambda l, j: (l, 0, j)),
        out_shape=jax.ShapeDtypeStruct((depth, rows, n), F32),
        compiler_params=_cparams(("arbitrary", "arbitrary")),
        name="modulation",
    )(cpad, w_mod, b_mod.reshape(depth, 1, n))


def _ffn_kernel(h_ref, mod_ref, wa_ref, wg_ref, wo_ref, lng_ref, lnb_ref, o_ref, u_scr, acc_scr, *, alpha):
    j = pl.program_id(1)

    @pl.when(j == 0)
    def _():
        m = mod_ref[0]
        u_scr[...] = (h_ref[...] * (1.0 + m[1:2]) + m[0:1]).astype(BF16)
        acc_scr[...] = jnp.zeros_like(acc_scr)

    u = u_scr[...]
    a = _dot(u, wa_ref[...])
    g = _dot(u, wg_ref[...])
    act = (_silu(g) * a).astype(BF16)
    acc_scr[...] += _dot(act, wo_ref[...])

    @pl.when(j == pl.num_programs(1) - 1)
    def _():
        m = mod_ref[0]
        x = alpha * h_ref[...] + m[2:3] * (0.5 * acc_scr[...])
        o_ref[...] = _layer_norm(x, lng_ref[...], lnb_ref[...])


def _ffn(h, n_rows, mod3, group_of_tile, w_in_bf, w_out_bf, ln_g, ln_b, alpha, tm, tf):
    d = h.shape[1]
    dff = w_out_bf.shape[0]
    nf = dff // tf
    return pl.pallas_call(
        functools.partial(_ffn_kernel, alpha=alpha),
        grid=(n_rows // tm, nf),
        in_specs=[pl.BlockSpec((tm, d), lambda i, j: (i, 0)),
                  pl.BlockSpec((1, 3, d), lambda i, j: (group_of_tile(i), 0, 0)),
                  pl.BlockSpec((d, tf), lambda i, j: (0, j)),
                  pl.BlockSpec((d, tf), lambda i, j: (0, j + nf)),
                  pl.BlockSpec((tf, d), lambda i, j: (j, 0)),
                  pl.BlockSpec((1, d), lambda i, j: (0, 0)),
                  pl.BlockSpec((1, d), lambda i, j: (0, 0))],
        out_specs=pl.BlockSpec((tm, d), lambda i, j: (i, 0)),
        out_shape=jax.ShapeDtypeStruct((n_rows, d), F32),
        scratch_shapes=[pltpu.VMEM((tm, d), BF16), pltpu.VMEM((tm, d), F32)],
        compiler_params=_cparams(("arbitrary", "arbitrary")),
        name="ffn",
    )(h, mod3, w_in_bf, w_in_bf, w_out_bf, ln_g.reshape(1, d), ln_b.reshape(1, d))


def _proj_kernel(h_ref, mod_ref, w_ref, o_ref, u_scr):
    @pl.when(pl.program_id(1) == 0)
    def _():
        m = mod_ref[0]
        u_scr[...] = (h_ref[...] * (1.0 + m[1:2]) + m[0:1]).astype(BF16)

    o_ref[...] = _dot(u_scr[...], w_ref[...])


def _proj(h, mod3, group_of_tile, w_big, tm, tn):
    n_rows, d = h.shape
    n = w_big.shape[1]
    return pl.pallas_call(
        _proj_kernel,
        grid=(n_rows // tm, n // tn),
        in_specs=[pl.BlockSpec((tm, d), lambda i, j: (i, 0)),
                  pl.BlockSpec((1, 3, d), lambda i, j: (group_of_tile(i), 0, 0)),
                  pl.BlockSpec((d, tn), lambda i, j: (0, j))],
        out_specs=pl.BlockSpec((tm, tn), lambda i, j: (i, j)),
        out_shape=jax.ShapeDtypeStruct((n_rows, n), F32),
        scratch_shapes=[pltpu.VMEM((tm, d), BF16)],
        compiler_params=_cparams(("arbitrary", "arbitrary")),
        name="in_proj",
    )(h, mod3, w_big)


def _conv_kernel(c_ref, p_ref, n_ref, w_ref, b_ref, o_ref, *, n_lat_blocks, blocks_per_seq, ctx_len):
    i = pl.program_id(0)
    tb = c_ref.shape[0]
    is_lat = i < n_lat_blocks
    in_seq = i % blocks_per_seq
    zero_prev = jnp.logical_and(is_lat, in_seq == 0)
    zero_next = jnp.logical_and(is_lat, in_seq == blocks_per_seq - 1)
    prev = jnp.where(zero_prev, 0.0, p_ref[...])
    nxt = jnp.where(zero_next, 0.0, n_ref[...])
    e = jnp.concatenate([prev, c_ref[...], nxt], axis=0)
    n_ext = e.shape[0]
    r = lax.broadcasted_iota(jnp.int32, (n_ext, 1), 0) - GRID_W
    ctx_off = (i - n_lat_blocks) * tb
    pos = jnp.where(is_lat, r & (GRID_W - 1), (r + ctx_off) & (ctx_len - 1))
    row_len = jnp.where(is_lat, GRID_W, ctx_len)
    e_m = jnp.where(pos != 0, pltpu.roll(e, 1, 0), 0.0)
    e_p = jnp.where(pos != row_len - 1, pltpu.roll(e, n_ext - 1, 0), 0.0)
    w = w_ref[...]
    vert = jnp.where(is_lat, 1.0, 0.0)
    acc = b_ref[...]
    for dr in range(3):
        lo = GRID_W * dr
        term = (w[3 * dr:3 * dr + 1] * e_m[lo:lo + tb] + w[3 * dr + 1:3 * dr + 2] * e[lo:lo + tb]
                + w[3 * dr + 2:3 * dr + 3] * e_p[lo:lo + tb])
        acc = acc + (term if dr == 1 else term * vert)
    o_ref[...] = _silu(acc)


def _conv(p, col0_blocks, w9, bias, n_lat_rows, seq, ctx_len, tc=512):
    n_rows = p.shape[0]
    ch = w9.shape[1]
    tb = SCAN_BLOCK
    halo_per_blk = tb // GRID_W
    n_halo_rows = n_rows // GRID_W
    kern = functools.partial(_conv_kernel, n_lat_blocks=n_lat_rows // tb, blocks_per_seq=seq // tb,
                             ctx_len=ctx_len)
    return pl.pallas_call(
        kern,
        grid=(n_rows // tb, ch // tc),
        in_specs=[pl.BlockSpec((tb, tc), lambda i, j: (i, col0_blocks + j)),
                  pl.BlockSpec((GRID_W, tc), lambda i, j: (jnp.maximum(i * halo_per_blk - 1, 0), col0_blocks + j)),
                  pl.BlockSpec((GRID_W, tc),
                               lambda i, j: (jnp.minimum((i + 1) * halo_per_blk, n_halo_rows - 1), col0_blocks + j)),
                  pl.BlockSpec((9, tc), lambda i, j: (0, j)),
                  pl.BlockSpec((1, tc), lambda i, j: (0, j))],
        out_specs=pl.BlockSpec((tb, tc), lambda i, j: (i, j)),
        out_shape=jax.ShapeDtypeStruct((n_rows, ch), F32),
        compiler_params=_cparams(("arbitrary", "arbitrary")),
        name="short_conv",
    )(p, p, p, w9, bias)


def _chunk_masks(tb, reverse):
    r = lax.broadcasted_iota(jnp.int32, (tb, tb), 0)
    c = lax.broadcasted_iota(jnp.int32, (tb, tb), 1)
    same = (r >> 6) == (c >> 6)
    tri = jnp.logical_and(same, (c >= r) if reverse else (c <= r))
    return tri.astype(F32), same.astype(F32)


def _causal(reverse):
    r = lax.broadcasted_iota(jnp.int32, (CHUNK, CHUNK), 0)
    c = lax.broadcasted_iota(jnp.int32, (CHUNK, CHUNK), 1)
    return (c >= r) if reverse else (c <= r)


def _chunk_order(tb, reverse):
    idx = list(range(tb // CHUNK))
    return idx[::-1] if reverse else idx


def _ssd_kernel(xs_ref, bm_ref, cm_ref, sm_ref, bias_ref, aneg_ref, y_ref, s_scr, *, d, reverse, heads, hdim):
    @pl.when(pl.program_id(1) == 0)
    def _():
        s_scr[...] = jnp.zeros_like(s_scr)

    tb = xs_ref.shape[0]
    hpg = heads // SSD_GROUPS
    gw = hpg * hdim
    tri, same = _chunk_masks(tb, reverse)
    dt = _softplus(sm_ref[...] + bias_ref[...])
    dta = dt * aneg_ref[...]
    acum = _dot_exact(tri, dta)
    aend = _dot_exact(same, dta)
    acum_t = acum.T
    k_i = lax.broadcasted_iota(jnp.int32, (SMALL_W, heads * hdim), 0)
    n_i = lax.broadcasted_iota(jnp.int32, (SMALL_W, heads * hdim), 1)
    expand = (k_i == d * heads + n_i // hdim).astype(F32)
    dt_e = _dot_exact(dt, expand)
    ac_e = _dot_exact(acum, expand)
    ae_e = _dot_exact(aend, expand)
    xs = xs_ref[...]
    xdt = xs * dt_e
    xw = (xdt * jnp.exp(ae_e - ac_e)).astype(BF16)
    xdt = xdt.astype(BF16)
    e_ac = jnp.exp(ac_e)
    e_ae = jnp.exp(ae_e)
    bm = bm_ref[...].astype(BF16)
    cm = cm_ref[...].astype(BF16)
    lmask = _causal(reverse)
    for ci in _chunk_order(tb, reverse):
        lo = ci * CHUNK
        sl = slice(lo, lo + CHUNK)
        for g in range(SSD_GROUPS):
            gs = slice(g * gw, (g + 1) * gw)
            cmg = cm[sl, g * SSD_STATE:(g + 1) * SSD_STATE]
            bmg = bm[sl, g * SSD_STATE:(g + 1) * SSD_STATE]
            cb = _dot_nt(cmg, bmg)
            s_g = s_scr[g]
            y_in = _dot(cmg, s_g.astype(BF16)) * e_ac[sl, gs]
            ys = []
            for h in range(hpg):
                hh = g * hpg + h
                col = acum[sl, d * heads + hh:d * heads + hh + 1]
                row = acum_t[d * heads + hh:d * heads + hh + 1, sl]
                m = cb * jnp.exp(jnp.where(lmask, col - row, -jnp.inf))
                ys.append(_dot(m.astype(BF16), xdt[sl, hh * hdim:(hh + 1) * hdim]))
            y_ref[sl, gs] = y_in + jnp.concatenate(ys, axis=1)
            s_scr[g] = s_g * e_ae[lo:lo + 1, gs] + _dot_tn(bmg, xw[sl, gs])


def _ml_kernel(q_ref, k_ref, v_ref, sm_ref, gb_ref, y_ref, c_scr, n_scr, m_scr, *, d, reverse, heads, gate_col0):
    @pl.when(pl.program_id(1) == 0)
    def _():
        c_scr[...] = jnp.zeros_like(c_scr)
        n_scr[...] = jnp.zeros_like(n_scr)
        m_scr[...] = jnp.zeros_like(m_scr)

    tb = q_ref.shape[0]
    hd = q_ref.shape[1] // heads
    tri, same = _chunk_masks(tb, reverse)
    raw = sm_ref[...] + gb_ref[...]
    lf = _log_sigmoid(raw)
    bcum = _dot_exact(tri, lf)
    bend = _dot_exact(same, lf)
    bcum_t = bcum.T
    raw_t = raw.T
    q_all = q_ref[...] * (hd ** -0.5)
    k_all = k_ref[...]
    v_all = v_ref[...]
    lmask = _causal(reverse)
    base = gate_col0 + d * 2 * heads
    for ci in _chunk_order(tb, reverse):
        lo = ci * CHUNK
        sl = slice(lo, lo + CHUNK)
        for h in range(heads):
            ic = base + h
            fc = base + heads + h
            hs = slice(h * hd, (h + 1) * hd)
            colb = bcum[sl, fc:fc + 1]
            coli = raw[sl, ic:ic + 1]
            rowb = bcum_t[fc:fc + 1, sl]
            rowi = raw_t[ic:ic + 1, sl]
            b_end = bend[lo:lo + 1, fc:fc + 1]
            qh = q_all[sl, hs]
            kh = k_all[sl, hs]
            vh = v_all[sl, hs]
            qb = qh.astype(BF16)
            kb = kh.astype(BF16)
            m_st = m_scr[h][:, 0:1]
            c_in = c_scr[h]
            n_in = n_scr[h]
            dmat = jnp.where(lmask, colb - rowb + rowi, -jnp.inf)
            gcol = colb + m_st
            m_t = jnp.maximum(jnp.max(dmat, axis=1, keepdims=True), gcol)
            s = jnp.exp(dmat - m_t) * _dot_nt(qb, kb)
            g_w = jnp.exp(gcol - m_t)
            num = _dot(s.astype(BF16), vh.astype(BF16)) + g_w * _dot_nt(qb, c_in.astype(BF16))
            den = jnp.sum(s, axis=1, keepdims=True) + g_w * jnp.sum(qh * n_in, axis=1, keepdims=True)
            y_ref[sl, hs] = num / jnp.maximum(jnp.abs(den), jnp.exp(-m_t))
            a = b_end - colb + coli
            m_loc = jnp.max(a, axis=0, keepdims=True)
            w = jnp.exp(a - m_loc)
            c_loc = _dot_tn((w * vh).astype(BF16), kb)
            n_loc = jnp.sum(w * kh, axis=0, keepdims=True)
            m_new = jnp.maximum(b_end + m_st, m_loc)
            f_old = jnp.exp(b_end + m_st - m_new)
            f_new = jnp.exp(m_loc - m_new)
            c_scr[h] = f_old * c_in + f_new * c_loc
            n_scr[h] = f_old * n_in + f_new * n_loc
            m_scr[h] = jnp.broadcast_to(m_new, m_scr.shape[1:])


def _gla_kernel(q_ref, k_ref, v_ref, sm_ref, w2_ref, b2_ref, y_ref, s_scr, *, reverse, heads):
    @pl.when(pl.program_id(1) == 0)
    def _():
        s_scr[...] = jnp.zeros_like(s_scr)

    tb = q_ref.shape[0]
    dk = q_ref.shape[1] // heads
    dv = v_ref.shape[1] // heads
    tri, same = _chunk_masks(tb, reverse)
    log_a = _log_sigmoid(_dot_exact(sm_ref[...], w2_ref[...]) + b2_ref[...]) * (1.0 / GLA_TAU)
    b = _dot_exact(tri, log_a)
    bend = _dot_exact(same, log_a)
    e_bend_t = jnp.exp(bend).T
    q_d = (q_ref[...] * (dk ** -0.5) * jnp.exp(b)).astype(BF16)
    k = k_ref[...]
    k_d = (k * jnp.exp(-b)).astype(BF16)
    kk = (k * jnp.exp(bend - b)).astype(BF16)
    v = v_ref[...].astype(BF16)
    lmask = _causal(reverse)
    for ci in _chunk_order(tb, reverse):
        lo = ci * CHUNK
        sl = slice(lo, lo + CHUNK)
        for h in range(heads):
            ks = slice(h * dk, (h + 1) * dk)
            vs = slice(h * dv, (h + 1) * dv)
            s_in = s_scr[h]
            att = jnp.where(lmask, _dot_nt(q_d[sl, ks], k_d[sl, ks]), 0.0)
            y_ref[sl, vs] = _dot(att.astype(BF16), v[sl, vs]) + _dot(q_d[sl, ks], s_in.astype(BF16))
            s_scr[h] = e_bend_t[ks, lo:lo + 1] * s_in + _dot_tn(kk[sl, ks], v[sl, vs])


def _scan_rows(batch, seq, ctx_len, reverse):
    tb = SCAN_BLOCK
    nc, nl = ctx_len // tb, seq // tb
    ctx_base = batch * seq // tb

    def rows(b, s):
        if reverse:
            return jnp.where(s < nc, ctx_base + b * nc + (nc - 1 - s), b * nl + (nl - 1 - (s - nc)))
        return jnp.where(s < nc, ctx_base + b * nc + s, b * nl + (s - nc))

    return rows, nc + nl


def _scan_call(kern, name, rows, grid, n_rows, ins, consts, out_w, scratch):
    tb = SCAN_BLOCK
    in_specs = [pl.BlockSpec((tb, w), functools.partial(lambda b, s, cb: (rows(b, s), cb), cb=cb))
                for _, w, cb in ins]
    in_specs += [pl.BlockSpec(c.shape, lambda b, s: (0, 0)) for c in consts]
    return pl.pallas_call(
        kern,
        grid=grid,
        in_specs=in_specs,
        out_specs=pl.BlockSpec((tb, out_w), lambda b, s: (rows(b, s), 0)),
        out_shape=jax.ShapeDtypeStruct((n_rows, out_w), F32),
        scratch_shapes=scratch,
        compiler_params=_cparams(("arbitrary", "arbitrary")),
        name=name,
    )(*[a for a, _, _ in ins], *consts)


def _group_norm(y, groups, center):
    gw = y.shape[1] // groups
    outs = []
    for g in range(groups):
        yg = y[:, g * gw:(g + 1) * gw]
        if center:
            yg = yg - jnp.mean(yg, axis=-1, keepdims=True)
        outs.append(yg * lax.rsqrt(jnp.mean(yg * yg, axis=-1, keepdims=True) + EPS))
    return jnp.concatenate(outs, axis=1)


def _merge_kernel(h_ref, mod_ref, gate_ref, z_ref, xs_ref, ysf_ref, ysb_ref, o_ref, ymf_ref, ymb_ref,
                  r_ref, ygf_ref, ygb_ref, dskip_ref, snw_ref, mnw_ref, gnw_ref, mb_ref,
                  wbs_ref, wbm_ref, wbg_ref, wo_ref, lng_ref, lnb_ref, out_ref, *, alpha, ml_heads):
    d = h_ref.shape[1]
    ys = (ysf_ref[...] + ysb_ref[...] + dskip_ref[...] * xs_ref[...]) * _silu(z_ref[...])
    ys = _group_norm(ys, SSD_GROUPS, False) * snw_ref[...]
    b_ssd = _dot(ys.astype(BF16), wbs_ref[...])
    ym = _sigmoid(o_ref[...]) * (ymf_ref[...] + ymb_ref[...])
    ym = _group_norm(ym, ml_heads, True) * mnw_ref[...]
    b_ml = _dot(ym.astype(BF16), wbm_ref[...])
    yg = _group_norm(ygf_ref[...] + ygb_ref[...], GLA_HEADS, True) * gnw_ref[...] * _silu(r_ref[...])
    b_gla = _dot(yg.astype(BF16), wbg_ref[...])
    mb = mb_ref[...]
    mix = (_sigmoid(gate_ref[:, 0:d] + mb[0:1]) * b_ssd
           + _sigmoid(gate_ref[:, d:2 * d] + mb[1:2]) * b_ml
           + _sigmoid(gate_ref[:, 2 * d:3 * d] + mb[2:3]) * b_gla)
    y = _dot(mix.astype(BF16), wo_ref[...])
    m = mod_ref[0]
    out_ref[...] = _layer_norm(alpha * h_ref[...] + m[2:3] * y, lng_ref[...], lnb_ref[...])


def kernel(x, c, ctx, c_ctx, w_mod, b_mod, ln_g, ln_b, ffn_w_in, ffn_w_out, w_in, merge_b, ssd_conv_w, ssd_conv_b, ssd_dt_bias, ssd_a_log, ssd_d, ssd_norm_w, ml_conv_w, ml_conv_b, ml_gate_b, ml_norm_w, gla_w2, gla_b2, gla_norm_w, w_br_ssd, w_br_ml, w_br_gla, w_out):
    batch, seq, d = x.shape
    ctx_len = ctx.shape[1]
    depth = w_mod.shape[0]
    alpha = (2 * depth) ** 0.25
    ssd_inner = ssd_norm_w.shape[1]
    ssd_heads = ssd_dt_bias.shape[2]
    ssd_hdim = ssd_inner // ssd_heads
    bc_w = 2 * SSD_GROUPS * SSD_STATE
    ml_w = ml_norm_w.shape[1]
    ml_heads = ml_gate_b.shape[3]
    gla_w = gla_norm_w.shape[1]
    gla_kw = gla_w2.shape[3]
    gla_rank = gla_w2.shape[2]
    n_lat = batch * seq
    n_all = n_lat + batch * ctx_len
    tb = SCAN_BLOCK
    assert seq % tb == 0 and ctx_len % tb == 0 and ctx_len & (ctx_len - 1) == 0
    assert 2 * ssd_heads + 4 * ml_heads + 2 * gla_rank <= SMALL_W

    tm = 512 if (seq % 512 == 0 and (batch * ctx_len) % 512 == 0) else tb
    tm_merge = 128
    tf = 512

    def group_of_tile_for(t):
        per_batch = seq // t
        return lambda i: jnp.minimum(i // per_batch, batch)

    assert (d, ssd_inner, ml_w, gla_w, gla_kw, bc_w) == (2048, 1024, 512, 512, 256, 512)
    gate_w = 3 * d
    col_z = gate_w
    col_mlqk = col_z + ssd_inner
    col_xs = col_mlqk + 2 * ml_w
    col_bc = col_xs + ssd_inner
    col_mlv = col_bc + bc_w
    col_mlo = col_mlv + ml_w
    col_glav = col_mlo + ml_w
    col_glar = col_glav + gla_w
    col_glaq = col_glar + gla_w
    col_glak = col_glaq + gla_kw
    col_small = col_glak + gla_kw
    tn_proj = 1280
    n_proj = -(-(col_small + SMALL_W) // tn_proj) * tn_proj
    conv_w = 2 * ml_w + ssd_inner + bc_w

    in_sizes = (ssd_inner, ssd_inner + bc_w, 2 * ssd_heads, ml_w, ml_w, ml_w, ml_w, 4 * ml_heads,
                gla_kw, gla_kw, gla_w, gla_w, 2 * gla_rank, gate_w)
    offs = [0]
    for sz in in_sizes:
        offs.append(offs[-1] + sz)
    (o_z, o_xbc, o_dt, o_mlq, o_mlk, o_mlv, o_mlo, o_mlg, o_gq, o_gk, o_gv, o_gr, o_lr, o_gate, o_end) = offs

    h = jnp.concatenate([x.reshape(n_lat, d), ctx.reshape(batch * ctx_len, d)], axis=0)

    rows_mod = 8
    cpad = jnp.concatenate([c, c_ctx[None], jnp.zeros((rows_mod - batch - 1, d), F32)], axis=0)
    mod = _modulation(cpad, w_mod, b_mod).reshape(depth, rows_mod, N_MOD, d)

    for l in range(depth):
        last = l == depth - 1
        n_out = n_lat if last else n_all

        ffn_in_bf = [ffn_w_in[l, k].astype(BF16) for k in range(2)]
        ffn_out_bf = [ffn_w_out[l, k].astype(BF16) for k in range(2)]
        wl = w_in[l]
        w_big = jnp.concatenate(
            [wl[:, o_gate:o_end], wl[:, o_z:o_xbc], wl[:, o_mlq:o_mlv], wl[:, o_xbc:o_dt],
             wl[:, o_mlv:o_mlg], wl[:, o_gv:o_lr], wl[:, o_gq:o_gv],
             wl[:, o_dt:o_mlq], wl[:, o_mlg:o_gq], wl[:, o_lr:o_gate],
             jnp.zeros((d, n_proj - (o_end - 0)), F32)], axis=1).astype(BF16)
        conv_w9 = jnp.concatenate([ml_conv_w[l].reshape(9, 2 * ml_w), ssd_conv_w[l].reshape(9, ssd_inner + bc_w)], axis=1)
        conv_b = jnp.concatenate([ml_conv_b[l], ssd_conv_b[l]])[None]

        def small_row(vals, start):
            return jnp.zeros((1, SMALL_W), F32).at[0, start:start + vals.shape[0]].set(vals)

        dt_bias_row = small_row(ssd_dt_bias[l].reshape(-1), 0)
        a_neg_row = small_row(-jnp.exp(ssd_a_log[l].astype(F32)).reshape(-1), 0)
        gate_b_row = small_row(ml_gate_b[l].reshape(-1), 2 * ssd_heads)
        lr0 = 2 * ssd_heads + 4 * ml_heads
        w2_pad = [jnp.zeros((SMALL_W, gla_kw), F32).at[lr0 + dd * gla_rank:lr0 + (dd + 1) * gla_rank].set(gla_w2[l, dd])
                  for dd in range(2)]

        h = _ffn(h, n_all, mod[l, :, 0:3], group_of_tile_for(tm), ffn_in_bf[0], ffn_out_bf[0], ln_g[l, 0], ln_b[l, 0],
                 alpha, tm, tf)

        p = _proj(h, mod[l, :, 3:6], group_of_tile_for(tm), w_big, tm, tn_proj)
        cv = _conv(p, col_mlqk // 512, conv_w9, conv_b, n_lat, seq, ctx_len)
        col_cvb = 2 * ml_w + ssd_inner
        ys, ym, yg = [], [], []
        for dd, reverse in ((0, False), (1, True)):
            rows, steps = _scan_rows(batch, seq, ctx_len, reverse)
            grid = (batch, steps)
            ys.append(_scan_call(
                functools.partial(_ssd_kernel, d=dd, reverse=reverse, heads=ssd_heads, hdim=ssd_hdim),
                "ssd_scan", rows, grid, n_all,
                [(cv, ssd_inner, 2 * ml_w // ssd_inner), (cv, bc_w // 2, col_cvb // (bc_w // 2)),
                 (cv, bc_w // 2, col_cvb // (bc_w // 2) + 1), (p, SMALL_W, col_small // SMALL_W)],
                [dt_bias_row, a_neg_row], ssd_inner,
                [pltpu.VMEM((SSD_GROUPS, SSD_STATE, ssd_inner // SSD_GROUPS), F32)]))
            ym.append(_scan_call(
                functools.partial(_ml_kernel, d=dd, reverse=reverse, heads=ml_heads, gate_col0=2 * ssd_heads),
                "mlstm_scan", rows, grid, n_all,
                [(cv, ml_w, 0), (cv, ml_w, 1), (p, ml_w, col_mlv // ml_w), (p, SMALL_W, col_small // SMALL_W)],
                [gate_b_row], ml_w,
                [pltpu.VMEM((ml_heads, ml_w // ml_heads, ml_w // ml_heads), F32),
                 pltpu.VMEM((ml_heads, 1, ml_w // ml_heads), F32),
                 pltpu.VMEM((ml_heads, 1, ml_w // ml_heads), F32)]))
            yg.append(_scan_call(
                functools.partial(_gla_kernel, reverse=reverse, heads=GLA_HEADS),
                "gla_scan", rows, grid, n_all,
                [(p, gla_kw, col_glaq // gla_kw), (p, gla_kw, col_glak // gla_kw), (p, gla_w, col_glav // gla_w),
                 (p, SMALL_W, col_small // SMALL_W)],
                [w2_pad[dd], gla_b2[l, dd][None]], gla_w,
                [pltpu.VMEM((GLA_HEADS, gla_kw // GLA_HEADS, gla_w // GLA_HEADS), F32)]))

        tmm = tm_merge
        row = lambda i: (i, 0)
        const = lambda i: (0, 0)

        def col(cb):
            return lambda i: (i, cb)

        g_of = group_of_tile_for(tmm)
        consts = [jnp.repeat(ssd_d[l], ssd_hdim)[None], ssd_norm_w[l][None], ml_norm_w[l][None], gla_norm_w[l][None],
                  merge_b[l], w_br_ssd[l].astype(BF16), w_br_ml[l].astype(BF16), w_br_gla[l].astype(BF16),
                  w_out[l].astype(BF16), ln_g[l, 1][None], ln_b[l, 1][None]]
        h_mix = pl.pallas_call(
            functools.partial(_merge_kernel, alpha=alpha, ml_heads=ml_heads),
            grid=(n_out // tmm,),
            in_specs=[pl.BlockSpec((tmm, d), row),
                      pl.BlockSpec((1, 3, d), lambda i: (g_of(i), 0, 0)),
                      pl.BlockSpec((tmm, gate_w), col(0)),
                      pl.BlockSpec((tmm, ssd_inner), col(col_z // ssd_inner)),
                      pl.BlockSpec((tmm, ssd_inner), col(2 * ml_w // ssd_inner)),
                      pl.BlockSpec((tmm, ssd_inner), row),
                      pl.BlockSpec((tmm, ssd_inner), row),
                      pl.BlockSpec((tmm, ml_w), col(col_mlo // ml_w)),
                      pl.BlockSpec((tmm, ml_w), row),
                      pl.BlockSpec((tmm, ml_w), row),
                      pl.BlockSpec((tmm, gla_w), col(col_glar // gla_w)),
                      pl.BlockSpec((tmm, gla_w), row),
                      pl.BlockSpec((tmm, gla_w), row)]
                     + [pl.BlockSpec(a.shape, const) for a in consts],
            out_specs=pl.BlockSpec((tmm, d), row),
            out_shape=jax.ShapeDtypeStruct((n_out, d), F32),
            compiler_params=_cparams(("arbitrary",)),
            name="merge_out",
        )(h, mod[l, :, 3:6], p, p, cv, ys[0], ys[1], p, ym[0], ym[1], p, yg[0], yg[1], *consts)

        h = _ffn(h_mix, n_out, mod[l, :, 6:9], group_of_tile_for(tm), ffn_in_bf[1], ffn_out_bf[1], ln_g[l, 2], ln_b[l, 2],
                 alpha, tm, tf)

    return h[:n_lat].reshape(batch, seq, d)
```

```python
import functools

import jax
import jax.numpy as jnp
from jax import lax
from jax.experimental import pallas as pl
from jax.experimental.pallas import tpu as pltpu

F32 = jnp.float32
BF16 = jnp.bfloat16
HIGHEST = lax.Precision.HIGHEST

GRID_W = 64
CHUNK = 64
SSD_GROUPS = 4
SSD_STATE = 64
GLA_HEADS = 4
GLA_TAU = 16.0
EPS = 1e-5
N_MOD = 9

SCAN_BLOCK = 256
SMALL_W = 128
V7X_VMEM_LIMIT = 56 * 1024 * 1024

NT_DIMS = (((1,), (1,)), ((), ()))
TN_DIMS = (((0,), (0,)), ((), ()))


def _cparams(sem, vmem=V7X_VMEM_LIMIT):
    return pltpu.CompilerParams(dimension_semantics=sem, vmem_limit_bytes=vmem)


def _dot(a, b):
    return jnp.dot(a, b, preferred_element_type=F32)


def _dot_exact(a, b):
    return jnp.dot(a, b, preferred_element_type=F32, precision=HIGHEST)


def _dot_nt(a, b):
    return lax.dot_general(a, b, NT_DIMS, preferred_element_type=F32)


def _dot_tn(a, b):
    return lax.dot_general(a, b, TN_DIMS, preferred_element_type=F32)


def _sigmoid(x):
    return 1.0 / (1.0 + jnp.exp(-x))


def _silu(x):
    return x * _sigmoid(x)


def _softplus(x):
    return jnp.maximum(x, 0.0) + jnp.log(1.0 + jnp.exp(-jnp.abs(x)))


def _log_sigmoid(x):
    return -_softplus(-x)


def _layer_norm(x, g, b):
    mu = jnp.mean(x, axis=-1, keepdims=True)
    xc = x - mu
    var = jnp.mean(xc * xc, axis=-1, keepdims=True)
    return xc * lax.rsqrt(var + EPS) * g + b


def _mod_kernel(c_ref, w_ref, b_ref, o_ref):
    c = c_ref[...]
    o_ref[0] = _dot_exact(_silu(c), w_ref[0]) + b_ref[0]


def _modulation(cpad, w_mod, b_mod, tn=1152):
    depth, d, n = w_mod.shape
    rows = cpad.shape[0]
    return pl.pallas_call(
        _mod_kernel,
        grid=(depth, n // tn),
        in_specs=[pl.BlockSpec((rows, d), lambda l, j: (0, 0)),
                  pl.BlockSpec((1, d, tn), lambda l, j: (l, 0, j)),
                  pl.BlockSpec((1, 1, tn), lambda l, j: (l, 0, j))],
        out_specs=pl.BlockSpec((1, rows, tn), lambda l, j: (l, 0, j)),
        out_shape=jax.ShapeDtypeStruct((depth, rows, n), F32),
        compiler_params=_cparams(("arbitrary", "arbitrary")),
        name="modulation",
    )(cpad, w_mod, b_mod.reshape(depth, 1, n))


def _ffn_kernel(h_ref, mod_ref, wa_ref, wg_ref, wo_ref, lng_ref, lnb_ref, o_ref, u_scr, acc_scr, *, alpha):
    j = pl.program_id(1)

    @pl.when(j == 0)
    def _():
        m = mod_ref[0]
        u_scr[...] = (h_ref[...] * (1.0 + m[1:2]) + m[0:1]).astype(BF16)
        acc_scr[...] = jnp.zeros_like(acc_scr)

    u = u_scr[...]
    a = _dot(u, wa_ref[...])
    g = _dot(u, wg_ref[...])
    act = (_silu(g) * a).astype(BF16)
    acc_scr[...] += _dot(act, wo_ref[...])

    @pl.when(j == pl.num_programs(1) - 1)
    def _():
        m = mod_ref[0]
        x = alpha * h_ref[...] + m[2:3] * (0.5 * acc_scr[...])
        o_ref[...] = _layer_norm(x, lng_ref[...], lnb_ref[...])


def _ffn_kernel_inplace(h_ref, mod_ref, wa_ref, wg_ref, wo_ref, lng_ref, lnb_ref, buf_ref, o_ref, u_scr, acc_scr, *,
                        alpha):
    del buf_ref
    _ffn_kernel(h_ref, mod_ref, wa_ref, wg_ref, wo_ref, lng_ref, lnb_ref, o_ref, u_scr, acc_scr, alpha=alpha)


def _ffn(h, n_rows, mod3, group_of_tile, w_in_bf, w_out_bf, lk, ln_g, ln_b, alpha, tm, tf, out_rows=None, tile0=0,
         out_buf=None):
    d = h.shape[1]
    dff = w_out_bf.shape[2]
    nf = dff // tf
    l, k = lk
    out_rows = n_rows if out_rows is None else out_rows
    in_specs = [pl.BlockSpec((tm, d), lambda i, j: (i, 0)),
                pl.BlockSpec((1, 3, d), lambda i, j: (group_of_tile(i + tile0), 0, 0)),
                pl.BlockSpec((None, None, d, tf), lambda i, j: (l, k, 0, j)),
                pl.BlockSpec((None, None, d, tf), lambda i, j: (l, k, 0, j + nf)),
                pl.BlockSpec((None, None, tf, d), lambda i, j: (l, k, j, 0)),
                pl.BlockSpec((1, d), lambda i, j: (0, 0)),
                pl.BlockSpec((1, d), lambda i, j: (0, 0))]
    args = [h, mod3, w_in_bf, w_in_bf, w_out_bf, ln_g.reshape(1, d), ln_b.reshape(1, d)]
    kern = functools.partial(_ffn_kernel, alpha=alpha)
    aliases = {}
    if out_buf is not None:
        in_specs.append(pl.BlockSpec(memory_space=pl.ANY))
        args.append(out_buf)
        aliases = {len(args) - 1: 0}
        kern = functools.partial(_ffn_kernel_inplace, alpha=alpha)
    return pl.pallas_call(
        kern,
        grid=(n_rows // tm, nf),
        in_specs=in_specs,
        out_specs=pl.BlockSpec((tm, d), lambda i, j: (i + tile0, 0)),
        out_shape=jax.ShapeDtypeStruct((out_rows, d), F32),
        scratch_shapes=[pltpu.VMEM((tm, d), BF16), pltpu.VMEM((tm, d), F32)],
        input_output_aliases=aliases,
        compiler_params=_cparams(("arbitrary", "arbitrary")),
        name="ffn",
    )(*args)


def _proj_kernel(h_ref, mod_ref, w_ref, o_ref, u_scr):
    @pl.when(pl.program_id(1) == 0)
    def _():
        m = mod_ref[0]
        u_scr[...] = (h_ref[...] * (1.0 + m[1:2]) + m[0:1]).astype(BF16)

    o_ref[...] = _dot(u_scr[...], w_ref[...])


def _proj(h, mod3, group_of_tile, w_big, tm, tn):
    n_rows, d = h.shape
    n = w_big.shape[1]
    return pl.pallas_call(
        _proj_kernel,
        grid=(n_rows // tm, n // tn),
        in_specs=[pl.BlockSpec((tm, d), lambda i, j: (i, 0)),
                  pl.BlockSpec((1, 3, d), lambda i, j: (group_of_tile(i), 0, 0)),
                  pl.BlockSpec((d, tn), lambda i, j: (0, j))],
        out_specs=pl.BlockSpec((tm, tn), lambda i, j: (i, j)),
        out_shape=jax.ShapeDtypeStruct((n_rows, n), F32),
        scratch_shapes=[pltpu.VMEM((tm, d), BF16)],
        compiler_params=_cparams(("arbitrary", "arbitrary")),
        name="in_proj",
    )(h, mod3, w_big)


def _conv_kernel(c_ref, p_ref, n_ref, w_ref, b_ref, o_ref, *, n_lat_blocks, blocks_per_seq):
    i = pl.program_id(0)
    tb = c_ref.shape[0]
    is_lat = i < n_lat_blocks
    in_seq = i % blocks_per_seq
    keep_prev = jnp.logical_and(is_lat, in_seq != 0)
    keep_next = jnp.logical_and(is_lat, in_seq != blocks_per_seq - 1)
    cur = c_ref[...]
    up = jnp.concatenate([jnp.where(keep_prev, p_ref[...], 0.0), cur[:tb - GRID_W]], axis=0)
    down = jnp.concatenate([cur[GRID_W:], jnp.where(keep_next, n_ref[...], 0.0)], axis=0)
    w = w_ref[...]
    vert = jnp.where(is_lat, 1.0, 0.0)
    w_up, w_mid, w_dn = w[0:3] * vert, w[3:6], w[6:9] * vert

    def column(dw):
        return w_up[dw:dw + 1] * up + w_mid[dw:dw + 1] * cur + w_dn[dw:dw + 1] * down

    r = lax.broadcasted_iota(jnp.int32, (tb, 1), 0)
    row_len = jnp.where(is_lat, GRID_W, tb)
    pos = r & (row_len - 1)
    acc = (column(1) + b_ref[...]
           + jnp.where(pos != 0, pltpu.roll(column(0), 1, 0), 0.0)
           + jnp.where(pos != row_len - 1, pltpu.roll(column(2), tb - 1, 0), 0.0))
    o_ref[...] = _silu(acc)


def _conv(p, col0_blocks, w9, bias, n_lat_rows, seq, tc):
    n_rows = p.shape[0]
    ch = w9.shape[1]
    tb = SCAN_BLOCK
    halo_per_blk = tb // GRID_W
    n_halo_rows = n_rows // GRID_W
    kern = functools.partial(_conv_kernel, n_lat_blocks=n_lat_rows // tb, blocks_per_seq=seq // tb)
    return pl.pallas_call(
        kern,
        grid=(n_rows // tb, ch // tc),
        in_specs=[pl.BlockSpec((tb, tc), lambda i, j: (i, col0_blocks + j)),
                  pl.BlockSpec((GRID_W, tc), lambda i, j: (jnp.maximum(i * halo_per_blk - 1, 0), col0_blocks + j)),
                  pl.BlockSpec((GRID_W, tc),
                               lambda i, j: (jnp.minimum((i + 1) * halo_per_blk, n_halo_rows - 1), col0_blocks + j)),
                  pl.BlockSpec((9, tc), lambda i, j: (0, j)),
                  pl.BlockSpec((1, tc), lambda i, j: (0, j))],
        out_specs=pl.BlockSpec((tb, tc), lambda i, j: (i, j)),
        out_shape=jax.ShapeDtypeStruct((n_rows, ch), F32),
        compiler_params=_cparams(("arbitrary", "arbitrary")),
        name="short_conv",
    )(p, p, p, w9, bias)


def _cumsum_op(tb, length, reverse):
    r = lax.broadcasted_iota(jnp.int32, (tb, tb), 0)
    c = lax.broadcasted_iota(jnp.int32, (tb, tb), 1)
    shift = length.bit_length() - 1
    same = (r >> shift) == (c >> shift)
    return jnp.logical_and(same, (c >= r) if reverse else (c <= r)).astype(BF16)


def _apply_op(op, x):
    hi = x.astype(BF16)
    r1 = x - hi.astype(F32)
    mid = r1.astype(BF16)
    lo = (r1 - mid.astype(F32)).astype(BF16)
    n = x.shape[1]
    out = _dot(op, jnp.concatenate([hi, mid, lo], axis=1))
    return out[:, 0:n] + out[:, n:2 * n] + out[:, 2 * n:3 * n]


def _causal(length, reverse):
    r = lax.broadcasted_iota(jnp.int32, (length, length), 0)
    c = lax.broadcasted_iota(jnp.int32, (length, length), 1)
    return (c >= r) if reverse else (c <= r)


def _chunk_order(tb, reverse):
    idx = list(range(tb // CHUNK))
    return idx[::-1] if reverse else idx


LANE = 128


def _half_masks():
    lane_lo = lax.broadcasted_iota(jnp.int32, (1, LANE), 1) < LANE // 2
    sub_lo = lax.broadcasted_iota(jnp.int32, (LANE, 1), 0) < LANE // 2
    return lane_lo, sub_lo


def _ssd_block(xs_ref, bm_ref, cm_ref, sm_ref, bias, aneg, y_ref, row0, s_ref, op, lmask, *, d, reverse, heads, hdim):
    tb = xs_ref.shape[0]
    hpg = heads // SSD_GROUPS
    assert hdim == LANE // 2 and SSD_STATE == LANE // 2 and hpg % 2 == 0 and SSD_GROUPS % 2 == 0
    end = 0 if reverse else tb - 1
    lane_lo, sub_lo = _half_masks()
    dt = _softplus(sm_ref[...] + bias)
    acum = _apply_op(op, dt * aneg)
    acum_t = acum.T
    dt_t = dt.T
    w_t = jnp.exp(acum_t[:, end:end + 1] - acum_t) * dt_t
    e_end = jnp.exp(acum[end:end + 1, :])
    xs = xs_ref[...].astype(BF16)
    bm_t = bm_ref[...].T
    cm = cm_ref[...].astype(BF16)
    for gp in range(SSD_GROUPS // 2):
        cm_pair = cm[:, gp * LANE:(gp + 1) * LANE]
        bm_pair_t = bm_t[gp * LANE:(gp + 1) * LANE, :]
        for gi in range(2):
            g = 2 * gp + gi
            in_g = sub_lo if gi == 0 else jnp.logical_not(sub_lo)
            cb = _dot(cm_pair, jnp.where(in_g, bm_pair_t, 0.0).astype(BF16))
            bmg_t = bm_t[g * SSD_STATE:(g + 1) * SSD_STATE, :]
            s_g = s_ref[g]
            zero = jnp.zeros_like(s_g)
            s_rows = jnp.concatenate([s_g, zero] if gi == 0 else [zero, s_g], axis=0).astype(BF16)
            y_in = _dot(cm_pair, s_rows)
            for pi in range(hpg // 2):
                gl = slice(pi * LANE, (pi + 1) * LANE)
                xl = slice(g * hpg * hdim + pi * LANE, g * hpg * hdim + (pi + 1) * LANE)
                xs_pair = xs[:, xl]
                xs_half = (jnp.where(lane_lo, xs_pair, 0.0).astype(BF16), jnp.where(lane_lo, 0.0, xs_pair).astype(BF16))
                c0 = d * heads + g * hpg + 2 * pi
                y_acc = None
                s_acc = None
                e_col = []
                for hi in range(2):
                    c = c0 + hi
                    colb = jnp.broadcast_to(acum[:, c:c + 1], (tb, tb))
                    m = cb * jnp.exp(jnp.where(lmask, colb - acum_t[c:c + 1, :], -jnp.inf)) * dt_t[c:c + 1, :]
                    y_h = _dot(m.astype(BF16), xs_half[hi])
                    s_h = _dot((bmg_t * w_t[c:c + 1, :]).astype(BF16), xs_half[hi])
                    y_acc = y_h if y_acc is None else y_acc + y_h
                    s_acc = s_h if s_acc is None else s_acc + s_h
                    e_col.append(jnp.exp(colb[:, 0:LANE]))
                y_ref[row0:row0 + tb, xl] = y_acc + y_in[:, gl] * jnp.where(lane_lo, e_col[0], e_col[1])
                e_pair = jnp.where(lane_lo, e_end[:, c0:c0 + 1], e_end[:, c0 + 1:c0 + 2])
                s_ref[g, :, gl] = s_g[:, gl] * e_pair + s_acc


def _ssd_kernel(*refs, batch, d, reverse, heads, hdim):
    ins, (bias_ref, aneg_ref, y_ref, s_scr) = refs[:4 * batch], refs[4 * batch:]

    @pl.when(pl.program_id(0) == 0)
    def _():
        s_scr[...] = jnp.zeros_like(s_scr)

    tb = SCAN_BLOCK
    op = _cumsum_op(tb, tb, reverse)
    lmask = _causal(tb, reverse)
    for b in range(batch):
        _ssd_block(*ins[4 * b:4 * b + 4], bias_ref[...], aneg_ref[...], y_ref, b * tb, s_scr.at[b], op, lmask,
                   d=d, reverse=reverse, heads=heads, hdim=hdim)


def _ml_kernel(*refs, batch, d, reverse, heads, gate_col0):
    ins, (gb_ref, y_ref, c_scr, n_scr, m_scr) = refs[:4 * batch], refs[4 * batch:]

    @pl.when(pl.program_id(0) == 0)
    def _():
        c_scr[...] = jnp.zeros_like(c_scr)
        n_scr[...] = jnp.zeros_like(n_scr)
        m_scr[...] = jnp.zeros_like(m_scr)

    tb = SCAN_BLOCK
    op = _cumsum_op(tb, tb, reverse)
    lmask = _causal(tb, reverse)
    for b in range(batch):
        _ml_block(*ins[4 * b:4 * b + 4], gb_ref[...], y_ref, b * tb, c_scr.at[b], n_scr.at[b], m_scr.at[b], op, lmask,
                  d=d, reverse=reverse, heads=heads, gate_col0=gate_col0)


def _ml_block(q_ref, k_ref, v_ref, sm_ref, gate_b, y_ref, row0, c_scr, n_scr, m_scr, op, lmask, *, d, reverse, heads,
              gate_col0):
    tb = q_ref.shape[0]
    hd = q_ref.shape[1] // heads
    end = 0 if reverse else tb - 1
    raw = sm_ref[...] + gate_b
    bcum = _apply_op(op, _log_sigmoid(raw))
    bcum_t = bcum.T
    raw_t = raw.T
    q_all = q_ref[...] * (hd ** -0.5)
    k_all = k_ref[...]
    v_all = v_ref[...]
    base = gate_col0 + d * 2 * heads
    for h in range(heads):
        ic = base + h
        fc = base + heads + h
        hs = slice(h * hd, (h + 1) * hd)
        colb = bcum[:, fc:fc + 1]
        coli = raw[:, ic:ic + 1]
        b_end = bcum[end:end + 1, fc:fc + 1]
        qh = q_all[:, hs]
        kh = k_all[:, hs]
        vh = v_all[:, hs]
        qb = qh.astype(BF16)
        kb = kh.astype(BF16)
        m_st = m_scr[h][:, 0:1]
        c_in = c_scr[h]
        n_in = n_scr[h]
        dmat = jnp.where(lmask, colb - bcum_t[fc:fc + 1, :] + raw_t[ic:ic + 1, :], -jnp.inf)
        gcol = colb + m_st
        m_t = jnp.maximum(jnp.max(dmat, axis=1, keepdims=True), gcol)
        s = jnp.exp(dmat - m_t) * _dot_nt(qb, kb)
        g_w = jnp.exp(gcol - m_t)
        num = _dot(s.astype(BF16), vh.astype(BF16)) + g_w * _dot_nt(qb, c_in.astype(BF16))
        den = jnp.sum(s, axis=1, keepdims=True) + g_w * jnp.sum(qh * n_in, axis=1, keepdims=True)
        y_ref[row0:row0 + tb, hs] = num / jnp.maximum(jnp.abs(den), jnp.exp(-m_t))
        a = b_end - colb + coli
        m_loc = jnp.max(a, axis=0, keepdims=True)
        w = jnp.exp(a - m_loc)
        c_loc = _dot_tn((w * vh).astype(BF16), kb)
        n_loc = jnp.sum(w * kh, axis=0, keepdims=True)
        m_new = jnp.maximum(b_end + m_st, m_loc)
        f_old = jnp.exp(b_end + m_st - m_new)
        f_new = jnp.exp(m_loc - m_new)
        c_scr[h] = f_old * c_in + f_new * c_loc
        n_scr[h] = f_old * n_in + f_new * n_loc
        m_scr[h] = jnp.broadcast_to(m_new, m_scr.shape[1:])


def _gla_block(q_ref, k_ref, v_ref, sm_ref, w2, b2, y_ref, row0, s_scr, op, lmask, *, reverse, heads):
    tb = q_ref.shape[0]
    dk = q_ref.shape[1] // heads
    dv = v_ref.shape[1] // heads
    assert dk == LANE // 2 and heads % 2 == 0
    lane_lo, _ = _half_masks()
    x = _dot(sm_ref[...].astype(BF16), w2) + b2
    b = _apply_op(op, _log_sigmoid(x) * (1.0 / GLA_TAU))
    q_d = (q_ref[...] * (dk ** -0.5) * jnp.exp(b)).astype(BF16)
    k = k_ref[...]
    k_d = k * jnp.exp(-b)
    v = v_ref[...].astype(BF16)
    state = [s_scr[h] for h in range(heads)]
    for ci in _chunk_order(tb, reverse):
        lo = ci * CHUNK
        sl = slice(lo, lo + CHUNK)
        end = lo if reverse else lo + CHUNK - 1
        b_end = b[end:end + 1, :]
        kk = k[sl] * jnp.exp(b_end - b[sl])
        e_end = jnp.exp(b_end)
        outs = []
        for pi in range(heads // 2):
            pl_ = slice(pi * LANE, (pi + 1) * LANE)
            q_pair = q_d[sl, pl_]
            for hi in range(2):
                h = 2 * pi + hi
                mine = lane_lo if hi == 0 else jnp.logical_not(lane_lo)
                vs = slice(h * dv, (h + 1) * dv)
                k_m = jnp.where(mine, k_d[sl, pl_], 0.0).astype(BF16)
                kk_m = jnp.where(mine, kk[:, pl_], 0.0).astype(BF16)
                att = jnp.where(lmask, _dot_nt(q_pair, k_m), 0.0)
                outs.append(_dot(att.astype(BF16), v[sl, vs]) + _dot_nt(q_pair, state[h].astype(BF16)))
                state[h] = state[h] * e_end[:, pl_] + _dot_tn(v[sl, vs], kk_m)
                yield
        y_ref[row0 + lo:row0 + lo + CHUNK, :] = jnp.concatenate(outs, axis=1)
    for h in range(heads):
        s_scr[h] = state[h]


def _interleave(gens):
    gens = list(gens)
    while gens:
        alive = []
        for g in gens:
            try:
                next(g)
                alive.append(g)
            except StopIteration:
                pass
        gens = alive


def _gla_kernel(*refs, batch, reverse, heads):
    ins, (w2_ref, b2_ref, y_ref, s_scr) = refs[:4 * batch], refs[4 * batch:]

    @pl.when(pl.program_id(0) == 0)
    def _():
        s_scr[...] = jnp.zeros_like(s_scr)

    tb = SCAN_BLOCK
    op = _cumsum_op(tb, CHUNK, reverse)
    lmask = _causal(CHUNK, reverse)
    _interleave(_gla_block(*ins[4 * b:4 * b + 4], w2_ref[...], b2_ref[...], y_ref, b * tb, s_scr.at[b], op, lmask,
                           reverse=reverse, heads=heads) for b in range(batch))


def _scan_rows(batch, seq, ctx_len, reverse):
    tb = SCAN_BLOCK
    nc, nl = ctx_len // tb, seq // tb
    ctx_base = batch * seq // tb

    def rows(b, s):
        if reverse:
            return jnp.where(s < nc, ctx_base + b * nc + (nc - 1 - s), b * nl + (nl - 1 - (s - nc)))
        return jnp.where(s < nc, ctx_base + b * nc + s, b * nl + (s - nc))

    return rows, nc + nl


def _scan_out_rows(batch, seq, ctx_len, reverse, tile):
    tb = SCAN_BLOCK
    nc, nl = ctx_len // tb, seq // tb
    sub = tb // tile

    def rows(i):
        q, part = i // sub, i % sub
        is_lat = q < batch * nl
        qc = jnp.maximum(q - batch * nl, 0)
        b = jnp.where(is_lat, q // nl, qc // nc)
        k = jnp.where(is_lat, q % nl, qc % nc)
        n = jnp.where(is_lat, nl, nc)
        s = jnp.where(is_lat, nc, 0) + (n - 1 - k if reverse else k)
        return (s * batch + b) * sub + part

    return rows


def _scan_call(kern, name, batch, rows, steps, ins, consts, out_w, scratch):
    tb = SCAN_BLOCK
    in_specs = [pl.BlockSpec((tb, w), functools.partial(lambda s, b, cb: (rows(b, s), cb), b=b, cb=cb))
                for b in range(batch) for _, w, cb in ins]
    in_specs += [pl.BlockSpec(c.shape, lambda s: (0, 0)) for c in consts]
    return pl.pallas_call(
        kern,
        grid=(steps,),
        in_specs=in_specs,
        out_specs=pl.BlockSpec((batch * tb, out_w), lambda s: (s, 0)),
        out_shape=jax.ShapeDtypeStruct((steps * batch * tb, out_w), F32),
        scratch_shapes=scratch,
        compiler_params=_cparams(("arbitrary",)),
        name=name,
    )(*[a for _ in range(batch) for a, _, _ in ins], *consts)


def _group_norm(y, groups, center):
    gw = y.shape[1] // groups
    outs = []
    for g in range(groups):
        yg = y[:, g * gw:(g + 1) * gw]
        if center:
            yg = yg - jnp.mean(yg, axis=-1, keepdims=True)
        outs.append(yg * lax.rsqrt(jnp.mean(yg * yg, axis=-1, keepdims=True) + EPS))
    return jnp.concatenate(outs, axis=1)


def _merge_kernel(h_ref, mod_ref, gate_ref, z_ref, xs_ref, ysf_ref, ysb_ref, o_ref, ymf_ref, ymb_ref,
                  r_ref, ygf_ref, ygb_ref, dskip_ref, snw_ref, mnw_ref, gnw_ref, mb_ref,
                  wbs_ref, wbm_ref, wbg_ref, wo_ref, lng_ref, lnb_ref, out_ref, *, alpha, ml_heads):
    d = h_ref.shape[1]
    ys = (ysf_ref[...] + ysb_ref[...] + dskip_ref[...] * xs_ref[...]) * _silu(z_ref[...])
    ys = _group_norm(ys, SSD_GROUPS, False) * snw_ref[...]
    b_ssd = _dot(ys.astype(BF16), wbs_ref[...])
    ym = _sigmoid(o_ref[...]) * (ymf_ref[...] + ymb_ref[...])
    ym = _group_norm(ym, ml_heads, True) * mnw_ref[...]
    b_ml = _dot(ym.astype(BF16), wbm_ref[...])
    yg = _group_norm(ygf_ref[...] + ygb_ref[...], GLA_HEADS, True) * gnw_ref[...] * _silu(r_ref[...])
    b_gla = _dot(yg.astype(BF16), wbg_ref[...])
    mb = mb_ref[...]
    mix = (_sigmoid(gate_ref[:, 0:d] + mb[0:1]) * b_ssd
           + _sigmoid(gate_ref[:, d:2 * d] + mb[1:2]) * b_ml
           + _sigmoid(gate_ref[:, 2 * d:3 * d] + mb[2:3]) * b_gla)
    y = _dot(mix.astype(BF16), wo_ref[...])
    m = mod_ref[0]
    out_ref[...] = _layer_norm(alpha * h_ref[...] + m[2:3] * y, lng_ref[...], lnb_ref[...])


def kernel(x, c, ctx, c_ctx, w_mod, b_mod, ln_g, ln_b, ffn_w_in, ffn_w_out, w_in, merge_b, ssd_conv_w, ssd_conv_b, ssd_dt_bias, ssd_a_log, ssd_d, ssd_norm_w, ml_conv_w, ml_conv_b, ml_gate_b, ml_norm_w, gla_w2, gla_b2, gla_norm_w, w_br_ssd, w_br_ml, w_br_gla, w_out):
    batch, seq, d = x.shape
    ctx_len = ctx.shape[1]
    depth = w_mod.shape[0]
    alpha = (2 * depth) ** 0.25
    ssd_inner = ssd_norm_w.shape[1]
    ssd_heads = ssd_dt_bias.shape[2]
    ssd_hdim = ssd_inner // ssd_heads
    bc_w = 2 * SSD_GROUPS * SSD_STATE
    ml_w = ml_norm_w.shape[1]
    ml_heads = ml_gate_b.shape[3]
    gla_w = gla_norm_w.shape[1]
    gla_kw = gla_w2.shape[3]
    gla_rank = gla_w2.shape[2]
    n_lat = batch * seq
    n_all = n_lat + batch * ctx_len
    tb = SCAN_BLOCK
    assert seq % tb == 0 and ctx_len == tb
    assert 2 * ssd_heads + 4 * ml_heads + 2 * gla_rank <= SMALL_W

    tm = 512 if (seq % 512 == 0 and (batch * ctx_len) % 512 == 0) else tb
    tm_merge = 128
    tf = 512

    def group_of_tile_for(t):
        per_batch = seq // t
        return lambda i: jnp.minimum(i // per_batch, batch)

    assert (d, ssd_inner, ml_w, gla_w, gla_kw, bc_w) == (2048, 1024, 512, 512, 256, 512)
    gate_w = 3 * d
    col_z = gate_w
    col_mlv = col_z + ssd_inner
    col_mlqk = col_mlv + ml_w
    col_xs = col_mlqk + 2 * ml_w
    col_bc = col_xs + ssd_inner
    col_mlo = col_bc + bc_w
    col_glav = col_mlo + ml_w
    col_glar = col_glav + gla_w
    col_glaq = col_glar + gla_w
    col_glak = col_glaq + gla_kw
    col_small = col_glak + gla_kw
    tn_proj = 1280
    n_proj = -(-(col_small + SMALL_W) // tn_proj) * tn_proj
    conv_w = 2 * ml_w + ssd_inner + bc_w
    tc_conv = conv_w // 2
    assert col_mlqk % tc_conv == 0

    in_sizes = (ssd_inner, ssd_inner + bc_w, 2 * ssd_heads, ml_w, ml_w, ml_w, ml_w, 4 * ml_heads,
                gla_kw, gla_kw, gla_w, gla_w, 2 * gla_rank, gate_w)
    offs = [0]
    for sz in in_sizes:
        offs.append(offs[-1] + sz)
    (o_z, o_xbc, o_dt, o_mlq, o_mlk, o_mlv, o_mlo, o_mlg, o_gq, o_gk, o_gv, o_gr, o_lr, o_gate, o_end) = offs

    rows_mod = 8
    cpad = jnp.concatenate([c, c_ctx[None], jnp.zeros((rows_mod - batch - 1, d), F32)], axis=0)
    mod = _modulation(cpad, w_mod, b_mod).reshape(depth, rows_mod, N_MOD, d)

    ffn_in_bf = ffn_w_in.astype(BF16)
    ffn_out_bf = ffn_w_out.astype(BF16)
    w_br_bf = [w.astype(BF16) for w in (w_br_ssd, w_br_ml, w_br_gla)]
    w_out_bf = w_out.astype(BF16)

    h = None
    for l in range(depth):
        last = l == depth - 1
        n_out = n_lat if last else n_all

        wl = w_in[l].astype(BF16)
        w_big = jnp.concatenate(
            [wl[:, o_gate:o_end], wl[:, o_z:o_xbc], wl[:, o_mlv:o_mlo], wl[:, o_mlq:o_mlv], wl[:, o_xbc:o_dt],
             wl[:, o_mlo:o_mlg], wl[:, o_gv:o_lr], wl[:, o_gq:o_gv],
             wl[:, o_dt:o_mlq], wl[:, o_mlg:o_gq], wl[:, o_lr:o_gate],
             jnp.zeros((d, n_proj - o_end), BF16)], axis=1)
        conv_w9 = jnp.concatenate([ml_conv_w[l].reshape(9, 2 * ml_w), ssd_conv_w[l].reshape(9, ssd_inner + bc_w)], axis=1)
        conv_b = jnp.concatenate([ml_conv_b[l], ssd_conv_b[l]])[None]

        def small_row(vals, start):
            return jnp.zeros((1, SMALL_W), F32).at[0, start:start + vals.shape[0]].set(vals)

        dt_bias_row = small_row(ssd_dt_bias[l].reshape(-1), 0)
        a_neg_row = small_row(-jnp.exp(ssd_a_log[l].astype(F32)).reshape(-1), 0)
        gate_b_row = small_row(ml_gate_b[l].reshape(-1), 2 * ssd_heads)
        lr0 = 2 * ssd_heads + 4 * ml_heads
        w2_pad = [jnp.zeros((SMALL_W, gla_kw), F32).at[lr0 + dd * gla_rank:lr0 + (dd + 1) * gla_rank].set(gla_w2[l, dd])
                  for dd in range(2)]

        ffn0 = functools.partial(_ffn, mod3=mod[l, :, 0:3], group_of_tile=group_of_tile_for(tm), w_in_bf=ffn_in_bf,
                                 w_out_bf=ffn_out_bf, lk=(l, 0), ln_g=ln_g[l, 0], ln_b=ln_b[l, 0], alpha=alpha,
                                 tm=tm, tf=tf)
        if h is None:
            h = ffn0(x.reshape(n_lat, d), n_lat, out_rows=n_all)
            h = ffn0(ctx.reshape(batch * ctx_len, d), batch * ctx_len, out_rows=n_all, tile0=n_lat // tm, out_buf=h)
        else:
            h = ffn0(h, n_all)

        p = _proj(h, mod[l, :, 3:6], group_of_tile_for(tm), w_big, tm, tn_proj)
        cv = _conv(p, col_mlqk // tc_conv, conv_w9, conv_b, n_lat, seq, tc_conv)
        col_cvb = 2 * ml_w + ssd_inner
        ys, ym, yg = [], [], []
        for dd, reverse in ((0, False), (1, True)):
            rows, steps = _scan_rows(batch, seq, ctx_len, reverse)
            ys.append(_scan_call(
                functools.partial(_ssd_kernel, batch=batch, d=dd, reverse=reverse, heads=ssd_heads, hdim=ssd_hdim),
                "ssd_scan", batch, rows, steps,
                [(cv, ssd_inner, 2 * ml_w // ssd_inner), (cv, bc_w // 2, col_cvb // (bc_w // 2)),
                 (cv, bc_w // 2, col_cvb // (bc_w // 2) + 1), (p, SMALL_W, col_small // SMALL_W)],
                [dt_bias_row, a_neg_row], ssd_inner,
                [pltpu.VMEM((batch, SSD_GROUPS, SSD_STATE, ssd_inner // SSD_GROUPS), F32)]))
            ym.append(_scan_call(
                functools.partial(_ml_kernel, batch=batch, d=dd, reverse=reverse, heads=ml_heads,
                                  gate_col0=2 * ssd_heads),
                "mlstm_scan", batch, rows, steps,
                [(cv, ml_w, 0), (cv, ml_w, 1), (p, ml_w, col_mlv // ml_w), (p, SMALL_W, col_small // SMALL_W)],
                [gate_b_row], ml_w,
                [pltpu.VMEM((batch, ml_heads, ml_w // ml_heads, ml_w // ml_heads), F32),
                 pltpu.VMEM((batch, ml_heads, 1, ml_w // ml_heads), F32),
                 pltpu.VMEM((batch, ml_heads, 1, ml_w // ml_heads), F32)]))
            yg.append(_scan_call(
                functools.partial(_gla_kernel, batch=batch, reverse=reverse, heads=GLA_HEADS),
                "gla_scan", batch, rows, steps,
                [(p, gla_kw, col_glaq // gla_kw), (p, gla_kw, col_glak // gla_kw), (p, gla_w, col_glav // gla_w),
                 (p, SMALL_W, col_small // SMALL_W)],
                [w2_pad[dd].astype(BF16), gla_b2[l, dd][None]], gla_w,
                [pltpu.VMEM((batch, GLA_HEADS, gla_w // GLA_HEADS, LANE), F32)]))

        tmm = tm_merge
        row = lambda i: (i, 0)
        const = lambda i: (0, 0)
        y_fwd = _scan_out_rows(batch, seq, ctx_len, False, tmm)
        y_bwd = _scan_out_rows(batch, seq, ctx_len, True, tmm)
        row_f = lambda i: (y_fwd(i), 0)
        row_b = lambda i: (y_bwd(i), 0)

        def col(cb):
            return lambda i: (i, cb)

        g_of = group_of_tile_for(tmm)
        consts = [jnp.repeat(ssd_d[l], ssd_hdim)[None], ssd_norm_w[l][None], ml_norm_w[l][None], gla_norm_w[l][None],
                  merge_b[l]]
        weights = [w_br_bf[0], w_br_bf[1], w_br_bf[2], w_out_bf]
        ln_consts = [ln_g[l, 1][None], ln_b[l, 1][None]]
        h_mix = pl.pallas_call(
            functools.partial(_merge_kernel, alpha=alpha, ml_heads=ml_heads),
            grid=(n_out // tmm,),
            in_specs=[pl.BlockSpec((tmm, d), row),
                      pl.BlockSpec((1, 3, d), lambda i: (g_of(i), 0, 0)),
                      pl.BlockSpec((tmm, gate_w), col(0)),
                      pl.BlockSpec((tmm, ssd_inner), col(col_z // ssd_inner)),
                      pl.BlockSpec((tmm, ssd_inner), col(2 * ml_w // ssd_inner)),
                      pl.BlockSpec((tmm, ssd_inner), row_f),
                      pl.BlockSpec((tmm, ssd_inner), row_b),
                      pl.BlockSpec((tmm, ml_w), col(col_mlo // ml_w)),
                      pl.BlockSpec((tmm, ml_w), row_f),
                      pl.BlockSpec((tmm, ml_w), row_b),
                      pl.BlockSpec((tmm, gla_w), col(col_glar // gla_w)),
                      pl.BlockSpec((tmm, gla_w), row_f),
                      pl.BlockSpec((tmm, gla_w), row_b)]
                     + [pl.BlockSpec(a.shape, const) for a in consts]
                     + [pl.BlockSpec((None,) + a.shape[1:], functools.partial(lambda i, l: (l, 0, 0), l=l))
                        for a in weights]
                     + [pl.BlockSpec(a.shape, const) for a in ln_consts],
            out_specs=pl.BlockSpec((tmm, d), row),
            out_shape=jax.ShapeDtypeStruct((n_out, d), F32),
            compiler_params=_cparams(("arbitrary",)),
            name="merge_out",
        )(h, mod[l, :, 3:6], p, p, cv, ys[0], ys[1], p, ym[0], ym[1], p, yg[0], yg[1], *consts, *weights, *ln_consts)

        h = _ffn(h_mix, n_out, mod[l, :, 6:9], group_of_tile_for(tm), ffn_in_bf, ffn_out_bf, (l, 1), ln_g[l, 2],
                 ln_b[l, 2], alpha, tm, tf)

    return h.reshape(batch, seq, d)
```

```python
import functools

import jax
import jax.numpy as jnp
from jax import lax
from jax.experimental import pallas as pl
from jax.experimental.pallas import tpu as pltpu

F32 = jnp.float32
BF16 = jnp.bfloat16
HIGHEST = lax.Precision.HIGHEST

GRID_W = 64
CHUNK = 64
SSD_GROUPS = 4
SSD_STATE = 64
GLA_HEADS = 4
GLA_TAU = 16.0
EPS = 1e-5
N_MOD = 9

SCAN_BLOCK = 256
SMALL_W = 128
V7X_VMEM_LIMIT = 56 * 1024 * 1024
V7X_VMEM_LIMIT_BIG = 60 * 1024 * 1024

NT_DIMS = (((1,), (1,)), ((), ()))
TN_DIMS = (((0,), (0,)), ((), ()))


def _cparams(sem, vmem=V7X_VMEM_LIMIT):
    return pltpu.CompilerParams(dimension_semantics=sem, vmem_limit_bytes=vmem)


def _dot(a, b):
    return jnp.dot(a, b, preferred_element_type=F32)


def _dot_exact(a, b):
    return jnp.dot(a, b, preferred_element_type=F32, precision=HIGHEST)


def _dot_nt(a, b):
    return lax.dot_general(a, b, NT_DIMS, preferred_element_type=F32)


def _dot_tn(a, b):
    return lax.dot_general(a, b, TN_DIMS, preferred_element_type=F32)


def _sigmoid(x):
    return 1.0 / (1.0 + jnp.exp(-x))


def _silu(x):
    return x * _sigmoid(x)


def _softplus(x):
    return jnp.maximum(x, 0.0) + jnp.log(1.0 + jnp.exp(-jnp.abs(x)))


def _log_sigmoid(x):
    return -_softplus(-x)


def _layer_norm(x, g, b):
    mu = jnp.mean(x, axis=-1, keepdims=True)
    xc = x - mu
    var = jnp.mean(xc * xc, axis=-1, keepdims=True)
    return xc * lax.rsqrt(var + EPS) * g + b


def _mod_kernel(c_ref, w_ref, b_ref, o_ref):
    c = c_ref[...]
    o_ref[0] = _dot_exact(_silu(c), w_ref[0]) + b_ref[0]


def _modulation(cpad, w_mod, b_mod, tn=1152):
    depth, d, n = w_mod.shape
    rows = cpad.shape[0]
    return pl.pallas_call(
        _mod_kernel,
        grid=(depth, n // tn),
        in_specs=[pl.BlockSpec((rows, d), lambda l, j: (0, 0)),
                  pl.BlockSpec((1, d, tn), lambda l, j: (l, 0, j)),
                  pl.BlockSpec((1, 1, tn), lambda l, j: (l, 0, j))],
        out_specs=pl.BlockSpec((1, rows, tn), lambda l, j: (l, 0, j)),
        out_shape=jax.ShapeDtypeStruct((depth, rows, n), F32),
        compiler_params=_cparams(("arbitrary", "arbitrary")),
        name="modulation",
    )(cpad, w_mod, b_mod.reshape(depth, 1, n))


def _ffn_kernel(h_ref, mod_ref, wa_ref, wg_ref, wo_ref, lng_ref, lnb_ref, o_ref, *, alpha):
    j = pl.program_id(1)
    m = mod_ref[0]
    u = (h_ref[...] * (1.0 + m[1:2]) + m[0:1]).astype(BF16)
    a = _dot(u, wa_ref[...])
    g = _dot(u, wg_ref[...])

    @pl.when(j == 0)
    def _():
        o_ref[...] = jnp.zeros_like(o_ref)

    o_ref[...] += _dot((_silu(g) * a).astype(BF16), wo_ref[...])

    @pl.when(j == pl.num_programs(1) - 1)
    def _():
        x = alpha * h_ref[...] + m[2:3] * (0.5 * o_ref[...])
        o_ref[...] = _layer_norm(x, lng_ref[...], lnb_ref[...])


def _ffn_kernel_inplace(h_ref, mod_ref, wa_ref, wg_ref, wo_ref, lng_ref, lnb_ref, buf_ref, o_ref, *, alpha):
    del buf_ref
    _ffn_kernel(h_ref, mod_ref, wa_ref, wg_ref, wo_ref, lng_ref, lnb_ref, o_ref, alpha=alpha)


def _ffn(h, n_rows, mod3, group_of_tile, w_in_bf, w_out_bf, lk, ln_g, ln_b, alpha, tm, tf, out_rows=None, tile0=0,
         out_buf=None):
    d = h.shape[1]
    dff = w_out_bf.shape[2]
    nf = dff // tf
    l, k = lk
    out_rows = n_rows if out_rows is None else out_rows
    in_specs = [pl.BlockSpec((tm, d), lambda i, j: (i, 0)),
                pl.BlockSpec((1, 3, d), lambda i, j: (group_of_tile(i + tile0), 0, 0)),
                pl.BlockSpec((None, None, d, tf), lambda i, j: (l, k, 0, j)),
                pl.BlockSpec((None, None, d, tf), lambda i, j: (l, k, 0, j + nf)),
                pl.BlockSpec((None, None, tf, d), lambda i, j: (l, k, j, 0)),
                pl.BlockSpec((1, d), lambda i, j: (0, 0)),
                pl.BlockSpec((1, d), lambda i, j: (0, 0))]
    args = [h, mod3, w_in_bf, w_in_bf, w_out_bf, ln_g.reshape(1, d), ln_b.reshape(1, d)]
    kern = functools.partial(_ffn_kernel, alpha=alpha)
    aliases = {}
    if out_buf is not None:
        in_specs.append(pl.BlockSpec(memory_space=pl.ANY))
        args.append(out_buf)
        aliases = {len(args) - 1: 0}
        kern = functools.partial(_ffn_kernel_inplace, alpha=alpha)
    return pl.pallas_call(
        kern,
        grid=(n_rows // tm, nf),
        in_specs=in_specs,
        out_specs=pl.BlockSpec((tm, d), lambda i, j: (i + tile0, 0)),
        out_shape=jax.ShapeDtypeStruct((out_rows, d), F32),
        input_output_aliases=aliases,
        compiler_params=_cparams(("arbitrary", "arbitrary"), V7X_VMEM_LIMIT_BIG),
        name="ffn",
    )(*args)


def _proj_kernel(h_ref, mod_ref, w_ref, o_ref, u_scr):
    @pl.when(pl.program_id(1) == 0)
    def _():
        m = mod_ref[0]
        u_scr[...] = (h_ref[...] * (1.0 + m[1:2]) + m[0:1]).astype(BF16)

    o_ref[...] = _dot(u_scr[...], w_ref[...])


def _proj(h, mod3, group_of_tile, w_big, tm, tn):
    n_rows, d = h.shape
    n = w_big.shape[1]
    return pl.pallas_call(
        _proj_kernel,
        grid=(n_rows // tm, n // tn),
        in_specs=[pl.BlockSpec((tm, d), lambda i, j: (i, 0)),
                  pl.BlockSpec((1, 3, d), lambda i, j: (group_of_tile(i), 0, 0)),
                  pl.BlockSpec((d, tn), lambda i, j: (0, j))],
        out_specs=pl.BlockSpec((tm, tn), lambda i, j: (i, j)),
        out_shape=jax.ShapeDtypeStruct((n_rows, n), F32),
        scratch_shapes=[pltpu.VMEM((tm, d), BF16)],
        compiler_params=_cparams(("arbitrary", "arbitrary")),
        name="in_proj",
    )(h, mod3, w_big)


def _conv_kernel(c_ref, p_ref, n_ref, w_ref, b_ref, o_ref, *, n_lat_blocks, blocks_per_seq):
    i = pl.program_id(0)
    tb = c_ref.shape[0]
    is_lat = i < n_lat_blocks
    in_seq = i % blocks_per_seq
    keep_prev = jnp.logical_and(is_lat, in_seq != 0)
    keep_next = jnp.logical_and(is_lat, in_seq != blocks_per_seq - 1)
    cur = c_ref[...]
    up = jnp.concatenate([jnp.where(keep_prev, p_ref[...], 0.0), cur[:tb - GRID_W]], axis=0)
    down = jnp.concatenate([cur[GRID_W:], jnp.where(keep_next, n_ref[...], 0.0)], axis=0)
    w = w_ref[...]
    vert = jnp.where(is_lat, 1.0, 0.0)
    w_up, w_mid, w_dn = w[0:3] * vert, w[3:6], w[6:9] * vert

    def column(dw):
        return w_up[dw:dw + 1] * up + w_mid[dw:dw + 1] * cur + w_dn[dw:dw + 1] * down

    r = lax.broadcasted_iota(jnp.int32, (tb, 1), 0)
    row_len = jnp.where(is_lat, GRID_W, tb)
    pos = r & (row_len - 1)
    acc = (column(1) + b_ref[...]
           + jnp.where(pos != 0, pltpu.roll(column(0), 1, 0), 0.0)
           + jnp.where(pos != row_len - 1, pltpu.roll(column(2), tb - 1, 0), 0.0))
    o_ref[...] = _silu(acc)


def _conv(p, col0_blocks, w9, bias, n_lat_rows, seq, tc):
    n_rows = p.shape[0]
    ch = w9.shape[1]
    tb = SCAN_BLOCK
    halo_per_blk = tb // GRID_W
    n_halo_rows = n_rows // GRID_W
    kern = functools.partial(_conv_kernel, n_lat_blocks=n_lat_rows // tb, blocks_per_seq=seq // tb)
    return pl.pallas_call(
        kern,
        grid=(n_rows // tb, ch // tc),
        in_specs=[pl.BlockSpec((tb, tc), lambda i, j: (i, col0_blocks + j)),
                  pl.BlockSpec((GRID_W, tc), lambda i, j: (jnp.maximum(i * halo_per_blk - 1, 0), col0_blocks + j)),
                  pl.BlockSpec((GRID_W, tc),
                               lambda i, j: (jnp.minimum((i + 1) * halo_per_blk, n_halo_rows - 1), col0_blocks + j)),
                  pl.BlockSpec((9, tc), lambda i, j: (0, j)),
                  pl.BlockSpec((1, tc), lambda i, j: (0, j))],
        out_specs=pl.BlockSpec((tb, tc), lambda i, j: (i, j)),
        out_shape=jax.ShapeDtypeStruct((n_rows, ch), F32),
        compiler_params=_cparams(("arbitrary", "arbitrary")),
        name="short_conv",
    )(p, p, p, w9, bias)


def _cumsum_op(tb, length, reverse):
    r = lax.broadcasted_iota(jnp.int32, (tb, tb), 0)
    c = lax.broadcasted_iota(jnp.int32, (tb, tb), 1)
    shift = length.bit_length() - 1
    same = (r >> shift) == (c >> shift)
    return jnp.logical_and(same, (c >= r) if reverse else (c <= r)).astype(BF16)


def _apply_op(op, x):
    hi = x.astype(BF16)
    r1 = x - hi.astype(F32)
    mid = r1.astype(BF16)
    lo = (r1 - mid.astype(F32)).astype(BF16)
    n = x.shape[1]
    out = _dot(op, jnp.concatenate([hi, mid, lo], axis=1))
    return out[:, 0:n] + out[:, n:2 * n] + out[:, 2 * n:3 * n]


def _causal(length, reverse):
    r = lax.broadcasted_iota(jnp.int32, (length, length), 0)
    c = lax.broadcasted_iota(jnp.int32, (length, length), 1)
    return (c >= r) if reverse else (c <= r)


def _chunk_order(tb, reverse):
    idx = list(range(tb // CHUNK))
    return idx[::-1] if reverse else idx


LANE = 128


def _half_masks():
    lane_lo = lax.broadcasted_iota(jnp.int32, (1, LANE), 1) < LANE // 2
    sub_lo = lax.broadcasted_iota(jnp.int32, (LANE, 1), 0) < LANE // 2
    return lane_lo, sub_lo


def _ssd_block(xs_ref, bm_ref, cm_ref, sm_ref, bias, aneg, y_ref, row0, s_ref, op, lmask, *, d, reverse, heads, hdim):
    tb = xs_ref.shape[0]
    hpg = heads // SSD_GROUPS
    assert hdim == LANE // 2 and SSD_STATE == LANE // 2 and hpg % 2 == 0 and SSD_GROUPS % 2 == 0
    end = 0 if reverse else tb - 1
    lane_lo, sub_lo = _half_masks()
    dt = _softplus(sm_ref[...] + bias)
    acum = _apply_op(op, dt * aneg)
    acum_t = acum.T
    dt_t = dt.T
    w_t = jnp.exp(acum_t[:, end:end + 1] - acum_t) * dt_t
    e_end = jnp.exp(acum[end:end + 1, :])
    xs = xs_ref[...].astype(BF16)
    bm_t = bm_ref[...].T
    cm = cm_ref[...].astype(BF16)
    for gp in range(SSD_GROUPS // 2):
        cm_pair = cm[:, gp * LANE:(gp + 1) * LANE]
        bm_pair_t = bm_t[gp * LANE:(gp + 1) * LANE, :]
        for gi in range(2):
            g = 2 * gp + gi
            in_g = sub_lo if gi == 0 else jnp.logical_not(sub_lo)
            cb = _dot(cm_pair, jnp.where(in_g, bm_pair_t, 0.0).astype(BF16))
            bmg_t = bm_t[g * SSD_STATE:(g + 1) * SSD_STATE, :]
            s_g = s_ref[g]
            zero = jnp.zeros_like(s_g)
            s_rows = jnp.concatenate([s_g, zero] if gi == 0 else [zero, s_g], axis=0).astype(BF16)
            y_in = _dot(cm_pair, s_rows)
            for pi in range(hpg // 2):
                gl = slice(pi * LANE, (pi + 1) * LANE)
                xl = slice(g * hpg * hdim + pi * LANE, g * hpg * hdim + (pi + 1) * LANE)
                xs_pair = xs[:, xl]
                xs_half = (jnp.where(lane_lo, xs_pair, 0.0).astype(BF16), jnp.where(lane_lo, 0.0, xs_pair).astype(BF16))
                c0 = d * heads + g * hpg + 2 * pi
                y_acc = None
                s_acc = None
                e_col = []
                for hi in range(2):
                    c = c0 + hi
                    colb = jnp.broadcast_to(acum[:, c:c + 1], (tb, tb))
                    m = cb * jnp.exp(jnp.where(lmask, colb - acum_t[c:c + 1, :], -jnp.inf)) * dt_t[c:c + 1, :]
                    y_h = _dot(m.astype(BF16), xs_half[hi])
                    s_h = _dot((bmg_t * w_t[c:c + 1, :]).astype(BF16), xs_half[hi])
                    y_acc = y_h if y_acc is None else y_acc + y_h
                    s_acc = s_h if s_acc is None else s_acc + s_h
                    e_col.append(jnp.exp(colb[:, 0:LANE]))
                y_ref[row0:row0 + tb, xl] = y_acc + y_in[:, gl] * jnp.where(lane_lo, e_col[0], e_col[1])
                e_pair = jnp.where(lane_lo, e_end[:, c0:c0 + 1], e_end[:, c0 + 1:c0 + 2])
                s_ref[g, :, gl] = s_g[:, gl] * e_pair + s_acc


def _ssd_kernel(*refs, batch, d, reverse, heads, hdim):
    ins, (bias_ref, aneg_ref, y_ref, s_scr) = refs[:4 * batch], refs[4 * batch:]

    @pl.when(pl.program_id(0) == 0)
    def _():
        s_scr[...] = jnp.zeros_like(s_scr)

    tb = SCAN_BLOCK
    op = _cumsum_op(tb, tb, reverse)
    lmask = _causal(tb, reverse)
    for b in range(batch):
        _ssd_block(*ins[4 * b:4 * b + 4], bias_ref[...], aneg_ref[...], y_ref, b * tb, s_scr.at[b], op, lmask,
                   d=d, reverse=reverse, heads=heads, hdim=hdim)


def _ml_kernel(*refs, batch, d, reverse, heads, gate_col0):
    ins, (gb_ref, y_ref, c_scr, n_scr, m_scr) = refs[:4 * batch], refs[4 * batch:]

    @pl.when(pl.program_id(0) == 0)
    def _():
        c_scr[...] = jnp.zeros_like(c_scr)
        n_scr[...] = jnp.zeros_like(n_scr)
        m_scr[...] = jnp.zeros_like(m_scr)

    tb = SCAN_BLOCK
    op = _cumsum_op(tb, tb, reverse)
    lmask = _causal(tb, reverse)
    for b in range(batch):
        _ml_block(*ins[4 * b:4 * b + 4], gb_ref[...], y_ref, b * tb, c_scr.at[b], n_scr.at[b], m_scr.at[b], op, lmask,
                  d=d, reverse=reverse, heads=heads, gate_col0=gate_col0)


def _ml_block(q_ref, k_ref, v_ref, sm_ref, gate_b, y_ref, row0, c_scr, n_scr, m_scr, op, lmask, *, d, reverse, heads,
              gate_col0):
    tb = q_ref.shape[0]
    hd = q_ref.shape[1] // heads
    end = 0 if reverse else tb - 1
    raw = sm_ref[...] + gate_b
    bcum = _apply_op(op, _log_sigmoid(raw))
    bcum_t = bcum.T
    raw_t = raw.T
    q_all = q_ref[...] * (hd ** -0.5)
    k_all = k_ref[...]
    v_all = v_ref[...]
    base = gate_col0 + d * 2 * heads
    for h in range(heads):
        ic = base + h
        fc = base + heads + h
        hs = slice(h * hd, (h + 1) * hd)
        colb = bcum[:, fc:fc + 1]
        coli = raw[:, ic:ic + 1]
        b_end = bcum[end:end + 1, fc:fc + 1]
        qh = q_all[:, hs]
        kh = k_all[:, hs]
        vh = v_all[:, hs]
        qb = qh.astype(BF16)
        kb = kh.astype(BF16)
        m_st = m_scr[h][:, 0:1]
        c_in = c_scr[h]
        n_in = n_scr[h]
        dmat = jnp.where(lmask, colb - bcum_t[fc:fc + 1, :] + raw_t[ic:ic + 1, :], -jnp.inf)
        gcol = colb + m_st
        m_t = jnp.maximum(jnp.max(dmat, axis=1, keepdims=True), gcol)
        s = jnp.exp(dmat - m_t) * _dot_nt(qb, kb)
        g_w = jnp.exp(gcol - m_t)
        num = _dot(s.astype(BF16), vh.astype(BF16)) + g_w * _dot_nt(qb, c_in.astype(BF16))
        den = jnp.sum(s, axis=1, keepdims=True) + g_w * jnp.sum(qh * n_in, axis=1, keepdims=True)
        y_ref[row0:row0 + tb, hs] = num / jnp.maximum(jnp.abs(den), jnp.exp(-m_t))
        a = b_end - colb + coli
        m_loc = jnp.max(a, axis=0, keepdims=True)
        w = jnp.exp(a - m_loc)
        c_loc = _dot_tn((w * vh).astype(BF16), kb)
        n_loc = jnp.sum(w * kh, axis=0, keepdims=True)
        m_new = jnp.maximum(b_end + m_st, m_loc)
        f_old = jnp.exp(b_end + m_st - m_new)
        f_new = jnp.exp(m_loc - m_new)
        c_scr[h] = f_old * c_in + f_new * c_loc
        n_scr[h] = f_old * n_in + f_new * n_loc
        m_scr[h] = jnp.broadcast_to(m_new, m_scr.shape[1:])


def _gla_block(q_ref, k_ref, v_ref, sm_ref, w2, b2, y_ref, row0, s_scr, op, lmask, *, reverse, heads):
    tb = q_ref.shape[0]
    dk = q_ref.shape[1] // heads
    dv = v_ref.shape[1] // heads
    assert dk == LANE // 2 and heads % 2 == 0
    lane_lo, _ = _half_masks()
    x = _dot(sm_ref[...].astype(BF16), w2) + b2
    b = _apply_op(op, _log_sigmoid(x) * (1.0 / GLA_TAU))
    q_d = (q_ref[...] * (dk ** -0.5) * jnp.exp(b)).astype(BF16)
    k = k_ref[...]
    k_d = k * jnp.exp(-b)
    v = v_ref[...].astype(BF16)
    state = [s_scr[h] for h in range(heads)]
    for ci in _chunk_order(tb, reverse):
        lo = ci * CHUNK
        sl = slice(lo, lo + CHUNK)
        end = lo if reverse else lo + CHUNK - 1
        b_end = b[end:end + 1, :]
        kk = k[sl] * jnp.exp(b_end - b[sl])
        e_end = jnp.exp(b_end)
        outs = []
        for pi in range(heads // 2):
            pl_ = slice(pi * LANE, (pi + 1) * LANE)
            q_pair = q_d[sl, pl_]
            for hi in range(2):
                h = 2 * pi + hi
                mine = lane_lo if hi == 0 else jnp.logical_not(lane_lo)
                vs = slice(h * dv, (h + 1) * dv)
                k_m = jnp.where(mine, k_d[sl, pl_], 0.0).astype(BF16)
                kk_m = jnp.where(mine, kk[:, pl_], 0.0).astype(BF16)
                att = jnp.where(lmask, _dot_nt(q_pair, k_m), 0.0)
                outs.append(_dot(att.astype(BF16), v[sl, vs]) + _dot_nt(q_pair, state[h].astype(BF16)))
                state[h] = state[h] * e_end[:, pl_] + _dot_tn(v[sl, vs], kk_m)
                yield
        y_ref[row0 + lo:row0 + lo + CHUNK, :] = jnp.concatenate(outs, axis=1)
    for h in range(heads):
        s_scr[h] = state[h]


def _interleave(gens):
    gens = list(gens)
    while gens:
        alive = []
        for g in gens:
            try:
                next(g)
                alive.append(g)
            except StopIteration:
                pass
        gens = alive


def _gla_kernel(*refs, batch, reverse, heads):
    ins, (w2_ref, b2_ref, y_ref, s_scr) = refs[:4 * batch], refs[4 * batch:]

    @pl.when(pl.program_id(0) == 0)
    def _():
        s_scr[...] = jnp.zeros_like(s_scr)

    tb = SCAN_BLOCK
    op = _cumsum_op(tb, CHUNK, reverse)
    lmask = _causal(CHUNK, reverse)
    _interleave(_gla_block(*ins[4 * b:4 * b + 4], w2_ref[...], b2_ref[...], y_ref, b * tb, s_scr.at[b], op, lmask,
                           reverse=reverse, heads=heads) for b in range(batch))


def _scan_rows(batch, seq, ctx_len, reverse):
    tb = SCAN_BLOCK
    nc, nl = ctx_len // tb, seq // tb
    ctx_base = batch * seq // tb

    def rows(b, s):
        if reverse:
            return jnp.where(s < nc, ctx_base + b * nc + (nc - 1 - s), b * nl + (nl - 1 - (s - nc)))
        return jnp.where(s < nc, ctx_base + b * nc + s, b * nl + (s - nc))

    return rows, nc + nl


def _scan_out_rows(batch, seq, ctx_len, reverse, tile):
    tb = SCAN_BLOCK
    nc, nl = ctx_len // tb, seq // tb
    sub = tb // tile

    def rows(i):
        q, part = i // sub, i % sub
        is_lat = q < batch * nl
        qc = jnp.maximum(q - batch * nl, 0)
        b = jnp.where(is_lat, q // nl, qc // nc)
        k = jnp.where(is_lat, q % nl, qc % nc)
        n = jnp.where(is_lat, nl, nc)
        s = jnp.where(is_lat, nc, 0) + (n - 1 - k if reverse else k)
        return (s * batch + b) * sub + part

    return rows


def _scan_call(kern, name, batch, rows, steps, ins, consts, out_w, scratch):
    tb = SCAN_BLOCK
    in_specs = [pl.BlockSpec((tb, w), functools.partial(lambda s, b, cb: (rows(b, s), cb), b=b, cb=cb))
                for b in range(batch) for _, w, cb in ins]
    in_specs += [pl.BlockSpec(c.shape, lambda s: (0, 0)) for c in consts]
    return pl.pallas_call(
        kern,
        grid=(steps,),
        in_specs=in_specs,
        out_specs=pl.BlockSpec((batch * tb, out_w), lambda s: (s, 0)),
        out_shape=jax.ShapeDtypeStruct((steps * batch * tb, out_w), F32),
        scratch_shapes=scratch,
        compiler_params=_cparams(("arbitrary",)),
        name=name,
    )(*[a for _ in range(batch) for a, _, _ in ins], *consts)


def _group_norm(y, groups, center):
    gw = y.shape[1] // groups
    outs = []
    for g in range(groups):
        yg = y[:, g * gw:(g + 1) * gw]
        if center:
            yg = yg - jnp.mean(yg, axis=-1, keepdims=True)
        outs.append(yg * lax.rsqrt(jnp.mean(yg * yg, axis=-1, keepdims=True) + EPS))
    return jnp.concatenate(outs, axis=1)


def _merge_kernel(h_ref, mod_ref, gate_ref, z_ref, xs_ref, ysf_ref, ysb_ref, o_ref, ymf_ref, ymb_ref,
                  r_ref, ygf_ref, ygb_ref, dskip_ref, snw_ref, mnw_ref, gnw_ref, mb_ref,
                  wbs_ref, wbm_ref, wbg_ref, wo_ref, lng_ref, lnb_ref, out_ref, *, alpha, ml_heads):
    d = h_ref.shape[1]
    ys = (ysf_ref[...] + ysb_ref[...] + dskip_ref[...] * xs_ref[...]) * _silu(z_ref[...])
    ys = _group_norm(ys, SSD_GROUPS, False) * snw_ref[...]
    b_ssd = _dot(ys.astype(BF16), wbs_ref[...])
    ym = _sigmoid(o_ref[...]) * (ymf_ref[...] + ymb_ref[...])
    ym = _group_norm(ym, ml_heads, True) * mnw_ref[...]
    b_ml = _dot(ym.astype(BF16), wbm_ref[...])
    yg = _group_norm(ygf_ref[...] + ygb_ref[...], GLA_HEADS, True) * gnw_ref[...] * _silu(r_ref[...])
    b_gla = _dot(yg.astype(BF16), wbg_ref[...])
    mb = mb_ref[...]
    mix = (_sigmoid(gate_ref[:, 0:d] + mb[0:1]) * b_ssd
           + _sigmoid(gate_ref[:, d:2 * d] + mb[1:2]) * b_ml
           + _sigmoid(gate_ref[:, 2 * d:3 * d] + mb[2:3]) * b_gla)
    y = _dot(mix.astype(BF16), wo_ref[...])
    m = mod_ref[0]
    out_ref[...] = _layer_norm(alpha * h_ref[...] + m[2:3] * y, lng_ref[...], lnb_ref[...])


def kernel(x, c, ctx, c_ctx, w_mod, b_mod, ln_g, ln_b, ffn_w_in, ffn_w_out, w_in, merge_b, ssd_conv_w, ssd_conv_b, ssd_dt_bias, ssd_a_log, ssd_d, ssd_norm_w, ml_conv_w, ml_conv_b, ml_gate_b, ml_norm_w, gla_w2, gla_b2, gla_norm_w, w_br_ssd, w_br_ml, w_br_gla, w_out):
    batch, seq, d = x.shape
    ctx_len = ctx.shape[1]
    depth = w_mod.shape[0]
    alpha = (2 * depth) ** 0.25
    ssd_inner = ssd_norm_w.shape[1]
    ssd_heads = ssd_dt_bias.shape[2]
    ssd_hdim = ssd_inner // ssd_heads
    bc_w = 2 * SSD_GROUPS * SSD_STATE
    ml_w = ml_norm_w.shape[1]
    ml_heads = ml_gate_b.shape[3]
    gla_w = gla_norm_w.shape[1]
    gla_kw = gla_w2.shape[3]
    gla_rank = gla_w2.shape[2]
    n_lat = batch * seq
    n_all = n_lat + batch * ctx_len
    tb = SCAN_BLOCK
    assert seq % tb == 0 and ctx_len == tb
    assert 2 * ssd_heads + 4 * ml_heads + 2 * gla_rank <= SMALL_W

    tm = next(t for t in (1024, 512, tb) if seq % t == 0 and (batch * ctx_len) % t == 0)
    tm_merge = 128
    tf = 512

    def group_of_tile_for(t):
        per_batch = seq // t
        return lambda i: jnp.minimum(i // per_batch, batch)

    assert (d, ssd_inner, ml_w, gla_w, gla_kw, bc_w) == (2048, 1024, 512, 512, 256, 512)
    gate_w = 3 * d
    col_z = gate_w
    col_mlv = col_z + ssd_inner
    col_mlqk = col_mlv + ml_w
    col_xs = col_mlqk + 2 * ml_w
    col_bc = col_xs + ssd_inner
    col_mlo = col_bc + bc_w
    col_glav = col_mlo + ml_w
    col_glar = col_glav + gla_w
    col_glaq = col_glar + gla_w
    col_glak = col_glaq + gla_kw
    col_small = col_glak + gla_kw
    tn_proj = 1280
    n_proj = -(-(col_small + SMALL_W) // tn_proj) * tn_proj
    conv_w = 2 * ml_w + ssd_inner + bc_w
    tc_conv = conv_w // 2
    assert col_mlqk % tc_conv == 0

    in_sizes = (ssd_inner, ssd_inner + bc_w, 2 * ssd_heads, ml_w, ml_w, ml_w, ml_w, 4 * ml_heads,
                gla_kw, gla_kw, gla_w, gla_w, 2 * gla_rank, gate_w)
    offs = [0]
    for sz in in_sizes:
        offs.append(offs[-1] + sz)
    (o_z, o_xbc, o_dt, o_mlq, o_mlk, o_mlv, o_mlo, o_mlg, o_gq, o_gk, o_gv, o_gr, o_lr, o_gate, o_end) = offs

    rows_mod = 8
    cpad = jnp.concatenate([c, c_ctx[None], jnp.zeros((rows_mod - batch - 1, d), F32)], axis=0)
    mod = _modulation(cpad, w_mod, b_mod).reshape(depth, rows_mod, N_MOD, d)

    ffn_in_bf = ffn_w_in.astype(BF16)
    ffn_out_bf = ffn_w_out.astype(BF16)
    w_br_bf = [w.astype(BF16) for w in (w_br_ssd, w_br_ml, w_br_gla)]
    w_out_bf = w_out.astype(BF16)

    h = None
    for l in range(depth):
        last = l == depth - 1
        n_out = n_lat if last else n_all

        wl = w_in[l].astype(BF16)
        w_big = jnp.concatenate(
            [wl[:, o_gate:o_end], wl[:, o_z:o_xbc], wl[:, o_mlv:o_mlo], wl[:, o_mlq:o_mlv], wl[:, o_xbc:o_dt],
             wl[:, o_mlo:o_mlg], wl[:, o_gv:o_lr], wl[:, o_gq:o_gv],
             wl[:, o_dt:o_mlq], wl[:, o_mlg:o_gq], wl[:, o_lr:o_gate],
             jnp.zeros((d, n_proj - o_end), BF16)], axis=1)
        conv_w9 = jnp.concatenate([ml_conv_w[l].reshape(9, 2 * ml_w), ssd_conv_w[l].reshape(9, ssd_inner + bc_w)], axis=1)
        conv_b = jnp.concatenate([ml_conv_b[l], ssd_conv_b[l]])[None]

        def small_row(vals, start):
            return jnp.zeros((1, SMALL_W), F32).at[0, start:start + vals.shape[0]].set(vals)

        dt_bias_row = small_row(ssd_dt_bias[l].reshape(-1), 0)
        a_neg_row = small_row(-jnp.exp(ssd_a_log[l].astype(F32)).reshape(-1), 0)
        gate_b_row = small_row(ml_gate_b[l].reshape(-1), 2 * ssd_heads)
        lr0 = 2 * ssd_heads + 4 * ml_heads
        w2_pad = [jnp.zeros((SMALL_W, gla_kw), F32).at[lr0 + dd * gla_rank:lr0 + (dd + 1) * gla_rank].set(gla_w2[l, dd])
                  for dd in range(2)]

        ffn0 = functools.partial(_ffn, mod3=mod[l, :, 0:3], group_of_tile=group_of_tile_for(tm), w_in_bf=ffn_in_bf,
                                 w_out_bf=ffn_out_bf, lk=(l, 0), ln_g=ln_g[l, 0], ln_b=ln_b[l, 0], alpha=alpha,
                                 tm=tm, tf=tf)
        if h is None:
            h = ffn0(x.reshape(n_lat, d), n_lat, out_rows=n_all)
            h = ffn0(ctx.reshape(batch * ctx_len, d), batch * ctx_len, out_rows=n_all, tile0=n_lat // tm, out_buf=h)
        else:
            h = ffn0(h, n_all)

        p = _proj(h, mod[l, :, 3:6], group_of_tile_for(tm), w_big, tm, tn_proj)
        cv = _conv(p, col_mlqk // tc_conv, conv_w9, conv_b, n_lat, seq, tc_conv)
        col_cvb = 2 * ml_w + ssd_inner
        ys, ym, yg = [], [], []
        for dd, reverse in ((0, False), (1, True)):
            rows, steps = _scan_rows(batch, seq, ctx_len, reverse)
            ys.append(_scan_call(
                functools.partial(_ssd_kernel, batch=batch, d=dd, reverse=reverse, heads=ssd_heads, hdim=ssd_hdim),
                "ssd_scan", batch, rows, steps,
                [(cv, ssd_inner, 2 * ml_w // ssd_inner), (cv, bc_w // 2, col_cvb // (bc_w // 2)),
                 (cv, bc_w // 2, col_cvb // (bc_w // 2) + 1), (p, SMALL_W, col_small // SMALL_W)],
                [dt_bias_row, a_neg_row], ssd_inner,
                [pltpu.VMEM((batch, SSD_GROUPS, SSD_STATE, ssd_inner // SSD_GROUPS), F32)]))
            ym.append(_scan_call(
                functools.partial(_ml_kernel, batch=batch, d=dd, reverse=reverse, heads=ml_heads,
                                  gate_col0=2 * ssd_heads),
                "mlstm_scan", batch, rows, steps,
                [(cv, ml_w, 0), (cv, ml_w, 1), (p, ml_w, col_mlv // ml_w), (p, SMALL_W, col_small // SMALL_W)],
                [gate_b_row], ml_w,
                [pltpu.VMEM((batch, ml_heads, ml_w // ml_heads, ml_w // ml_heads), F32),
                 pltpu.VMEM((batch, ml_heads, 1, ml_w // ml_heads), F32),
                 pltpu.VMEM((batch, ml_heads, 1, ml_w // ml_heads), F32)]))
            yg.append(_scan_call(
                functools.partial(_gla_kernel, batch=batch, reverse=reverse, heads=GLA_HEADS),
                "gla_scan", batch, rows, steps,
                [(p, gla_kw, col_glaq // gla_kw), (p, gla_kw, col_glak // gla_kw), (p, gla_w, col_glav // gla_w),
                 (p, SMALL_W, col_small // SMALL_W)],
                [w2_pad[dd].astype(BF16), gla_b2[l, dd][None]], gla_w,
                [pltpu.VMEM((batch, GLA_HEADS, gla_w // GLA_HEADS, LANE), F32)]))

        tmm = tm_merge
        row = lambda i: (i, 0)
        const = lambda i: (0, 0)
        y_fwd = _scan_out_rows(batch, seq, ctx_len, False, tmm)
        y_bwd = _scan_out_rows(batch, seq, ctx_len, True, tmm)
        row_f = lambda i: (y_fwd(i), 0)
        row_b = lambda i: (y_bwd(i), 0)

        def col(cb):
            return lambda i: (i, cb)

        g_of = group_of_tile_for(tmm)
        consts = [jnp.repeat(ssd_d[l], ssd_hdim)[None], ssd_norm_w[l][None], ml_norm_w[l][None], gla_norm_w[l][None],
                  merge_b[l]]
        weights = [w_br_bf[0], w_br_bf[1], w_br_bf[2], w_out_bf]
        ln_consts = [ln_g[l, 1][None], ln_b[l, 1][None]]
        h_mix = pl.pallas_call(
            functools.partial(_merge_kernel, alpha=alpha, ml_heads=ml_heads),
            grid=(n_out // tmm,),
            in_specs=[pl.BlockSpec((tmm, d), row),
                      pl.BlockSpec((1, 3, d), lambda i: (g_of(i), 0, 0)),
                      pl.BlockSpec((tmm, gate_w), col(0)),
                      pl.BlockSpec((tmm, ssd_inner), col(col_z // ssd_inner)),
                      pl.BlockSpec((tmm, ssd_inner), col(2 * ml_w // ssd_inner)),
                      pl.BlockSpec((tmm, ssd_inner), row_f),
                      pl.BlockSpec((tmm, ssd_inner), row_b),
                      pl.BlockSpec((tmm, ml_w), col(col_mlo // ml_w)),
                      pl.BlockSpec((tmm, ml_w), row_f),
                      pl.BlockSpec((tmm, ml_w), row_b),
                      pl.BlockSpec((tmm, gla_w), col(col_glar // gla_w)),
                      pl.BlockSpec((tmm, gla_w), row_f),
                      pl.BlockSpec((tmm, gla_w), row_b)]
                     + [pl.BlockSpec(a.shape, const) for a in consts]
                     + [pl.BlockSpec((None,) + a.shape[1:], functools.partial(lambda i, l: (l, 0, 0), l=l))
                        for a in weights]
                     + [pl.BlockSpec(a.shape, const) for a in ln_consts],
            out_specs=pl.BlockSpec((tmm, d), row),
            out_shape=jax.ShapeDtypeStruct((n_out, d), F32),
            compiler_params=_cparams(("arbitrary",)),
            name="merge_out",
        )(h, mod[l, :, 3:6], p, p, cv, ys[0], ys[1], p, ym[0], ym[1], p, yg[0], yg[1], *consts, *weights, *ln_consts)

        h = _ffn(h_mix, n_out, mod[l, :, 6:9], group_of_tile_for(tm), ffn_in_bf, ffn_out_bf, (l, 1), ln_g[l, 2],
                 ln_b[l, 2], alpha, tm, tf)

    return h.reshape(batch, seq, d)
```

```python
import functools

import jax
import jax.numpy as jnp
from jax import lax
from jax.experimental import pallas as pl
from jax.experimental.pallas import tpu as pltpu

F32 = jnp.float32
BF16 = jnp.bfloat16
HIGHEST = lax.Precision.HIGHEST

GRID_W = 64
CHUNK = 64
SSD_GROUPS = 4
SSD_STATE = 64
GLA_HEADS = 4
GLA_TAU = 16.0
EPS = 1e-5
N_MOD = 9

SCAN_BLOCK = 256
SMALL_W = 128
V7X_VMEM_LIMIT = 56 * 1024 * 1024
V7X_VMEM_LIMIT_BIG = 60 * 1024 * 1024

NT_DIMS = (((1,), (1,)), ((), ()))
TN_DIMS = (((0,), (0,)), ((), ()))


def _cparams(sem, vmem=V7X_VMEM_LIMIT):
    return pltpu.CompilerParams(dimension_semantics=sem, vmem_limit_bytes=vmem)


def _dot(a, b):
    return jnp.dot(a, b, preferred_element_type=F32)


def _dot_exact(a, b):
    return jnp.dot(a, b, preferred_element_type=F32, precision=HIGHEST)


def _dot_nt(a, b):
    return lax.dot_general(a, b, NT_DIMS, preferred_element_type=F32)


def _dot_tn(a, b):
    return lax.dot_general(a, b, TN_DIMS, preferred_element_type=F32)


def _sigmoid(x):
    return 1.0 / (1.0 + jnp.exp(-x))


def _silu(x):
    return x * _sigmoid(x)


def _softplus(x):
    return jnp.maximum(x, 0.0) + jnp.log(1.0 + jnp.exp(-jnp.abs(x)))


def _log_sigmoid(x):
    return -_softplus(-x)


def _layer_norm(x, g, b):
    mu = jnp.mean(x, axis=-1, keepdims=True)
    xc = x - mu
    var = jnp.mean(xc * xc, axis=-1, keepdims=True)
    return xc * lax.rsqrt(var + EPS) * g + b


def _mod_kernel(c_ref, w_ref, b_ref, o_ref):
    c = c_ref[...]
    o_ref[0] = _dot_exact(_silu(c), w_ref[0]) + b_ref[0]


def _modulation(cpad, w_mod, b_mod, tn=1152):
    depth, d, n = w_mod.shape
    rows = cpad.shape[0]
    return pl.pallas_call(
        _mod_kernel,
        grid=(depth, n // tn),
        in_specs=[pl.BlockSpec((rows, d), lambda l, j: (0, 0)),
                  pl.BlockSpec((1, d, tn), lambda l, j: (l, 0, j)),
                  pl.BlockSpec((1, 1, tn), lambda l, j: (l, 0, j))],
        out_specs=pl.BlockSpec((1, rows, tn), lambda l, j: (l, 0, j)),
        out_shape=jax.ShapeDtypeStruct((depth, rows, n), F32),
        compiler_params=_cparams(("arbitrary", "arbitrary")),
        name="modulation",
    )(cpad, w_mod, b_mod.reshape(depth, 1, n))


def _ffn_kernel(h_ref, mod_ref, wa_ref, wg_ref, wo_ref, lng_ref, lnb_ref, o_ref, *, alpha):
    j = pl.program_id(1)
    m = mod_ref[0]
    u = (h_ref[...] * (1.0 + m[1:2]) + m[0:1]).astype(BF16)
    a = _dot(u, wa_ref[...])
    g = _dot(u, wg_ref[...])

    @pl.when(j == 0)
    def _():
        o_ref[...] = jnp.zeros_like(o_ref)

    o_ref[...] += _dot((_silu(g) * a).astype(BF16), wo_ref[...])

    @pl.when(j == pl.num_programs(1) - 1)
    def _():
        x = alpha * h_ref[...] + m[2:3] * (0.5 * o_ref[...])
        o_ref[...] = _layer_norm(x, lng_ref[...], lnb_ref[...])


def _ffn_kernel_inplace(h_ref, mod_ref, wa_ref, wg_ref, wo_ref, lng_ref, lnb_ref, buf_ref, o_ref, *, alpha):
    del buf_ref
    _ffn_kernel(h_ref, mod_ref, wa_ref, wg_ref, wo_ref, lng_ref, lnb_ref, o_ref, alpha=alpha)


def _ffn(h, n_rows, mod3, group_of_tile, w_in_bf, w_out_bf, lk, ln_g, ln_b, alpha, tm, tf, out_rows=None, tile0=0,
         out_buf=None):
    d = h.shape[1]
    dff = w_out_bf.shape[2]
    nf = dff // tf
    l, k = lk
    out_rows = n_rows if out_rows is None else out_rows
    in_specs = [pl.BlockSpec((tm, d), lambda i, j: (i, 0)),
                pl.BlockSpec((1, 3, d), lambda i, j: (group_of_tile(i + tile0), 0, 0)),
                pl.BlockSpec((None, None, d, tf), lambda i, j: (l, k, 0, j)),
                pl.BlockSpec((None, None, d, tf), lambda i, j: (l, k, 0, j + nf)),
                pl.BlockSpec((None, None, tf, d), lambda i, j: (l, k, j, 0)),
                pl.BlockSpec((1, d), lambda i, j: (0, 0)),
                pl.BlockSpec((1, d), lambda i, j: (0, 0))]
    args = [h, mod3, w_in_bf, w_in_bf, w_out_bf, ln_g.reshape(1, d), ln_b.reshape(1, d)]
    kern = functools.partial(_ffn_kernel, alpha=alpha)
    aliases = {}
    if out_buf is not None:
        in_specs.append(pl.BlockSpec(memory_space=pl.ANY))
        args.append(out_buf)
        aliases = {len(args) - 1: 0}
        kern = functools.partial(_ffn_kernel_inplace, alpha=alpha)
    return pl.pallas_call(
        kern,
        grid=(n_rows // tm, nf),
        in_specs=in_specs,
        out_specs=pl.BlockSpec((tm, d), lambda i, j: (i + tile0, 0)),
        out_shape=jax.ShapeDtypeStruct((out_rows, d), F32),
        input_output_aliases=aliases,
        compiler_params=_cparams(("arbitrary", "arbitrary"), V7X_VMEM_LIMIT_BIG),
        name="ffn",
    )(*args)


def _proj_kernel(h_ref, mod_ref, w_ref, o_ref, u_scr):
    @pl.when(pl.program_id(1) == 0)
    def _():
        m = mod_ref[0]
        u_scr[...] = (h_ref[...] * (1.0 + m[1:2]) + m[0:1]).astype(BF16)

    o_ref[...] = _dot(u_scr[...], w_ref[...])


def _proj(h, mod3, group_of_tile, w_big, tm, tn):
    n_rows, d = h.shape
    n = w_big.shape[1]
    return pl.pallas_call(
        _proj_kernel,
        grid=(n_rows // tm, n // tn),
        in_specs=[pl.BlockSpec((tm, d), lambda i, j: (i, 0)),
                  pl.BlockSpec((1, 3, d), lambda i, j: (group_of_tile(i), 0, 0)),
                  pl.BlockSpec((d, tn), lambda i, j: (0, j))],
        out_specs=pl.BlockSpec((tm, tn), lambda i, j: (i, j)),
        out_shape=jax.ShapeDtypeStruct((n_rows, n), F32),
        scratch_shapes=[pltpu.VMEM((tm, d), BF16)],
        compiler_params=_cparams(("arbitrary", "arbitrary")),
        name="in_proj",
    )(h, mod3, w_big)


def _conv_kernel(c_ref, p_ref, n_ref, w_ref, b_ref, o_ref, *, n_lat_blocks, blocks_per_seq):
    i = pl.program_id(0)
    tb = c_ref.shape[0]
    is_lat = i < n_lat_blocks
    in_seq = i % blocks_per_seq
    keep_prev = jnp.logical_and(is_lat, in_seq != 0)
    keep_next = jnp.logical_and(is_lat, in_seq != blocks_per_seq - 1)
    cur = c_ref[...]
    up = jnp.concatenate([jnp.where(keep_prev, p_ref[...], 0.0), cur[:tb - GRID_W]], axis=0)
    down = jnp.concatenate([cur[GRID_W:], jnp.where(keep_next, n_ref[...], 0.0)], axis=0)
    w = w_ref[...]
    vert = jnp.where(is_lat, 1.0, 0.0)
    w_up, w_mid, w_dn = w[0:3] * vert, w[3:6], w[6:9] * vert

    def column(dw):
        return w_up[dw:dw + 1] * up + w_mid[dw:dw + 1] * cur + w_dn[dw:dw + 1] * down

    r = lax.broadcasted_iota(jnp.int32, (tb, 1), 0)
    row_len = jnp.where(is_lat, GRID_W, tb)
    pos = r & (row_len - 1)
    acc = (column(1) + b_ref[...]
           + jnp.where(pos != 0, pltpu.roll(column(0), 1, 0), 0.0)
           + jnp.where(pos != row_len - 1, pltpu.roll(column(2), tb - 1, 0), 0.0))
    o_ref[...] = _silu(acc)


def _conv(p, col0_blocks, w9, bias, n_lat_rows, seq, tc):
    n_rows = p.shape[0]
    ch = w9.shape[1]
    tb = SCAN_BLOCK
    halo_per_blk = tb // GRID_W
    n_halo_rows = n_rows // GRID_W
    kern = functools.partial(_conv_kernel, n_lat_blocks=n_lat_rows // tb, blocks_per_seq=seq // tb)
    return pl.pallas_call(
        kern,
        grid=(n_rows // tb, ch // tc),
        in_specs=[pl.BlockSpec((tb, tc), lambda i, j: (i, col0_blocks + j)),
                  pl.BlockSpec((GRID_W, tc), lambda i, j: (jnp.maximum(i * halo_per_blk - 1, 0), col0_blocks + j)),
                  pl.BlockSpec((GRID_W, tc),
                               lambda i, j: (jnp.minimum((i + 1) * halo_per_blk, n_halo_rows - 1), col0_blocks + j)),
                  pl.BlockSpec((9, tc), lambda i, j: (0, j)),
                  pl.BlockSpec((1, tc), lambda i, j: (0, j))],
        out_specs=pl.BlockSpec((tb, tc), lambda i, j: (i, j)),
        out_shape=jax.ShapeDtypeStruct((n_rows, ch), F32),
        compiler_params=_cparams(("arbitrary", "arbitrary")),
        name="short_conv",
    )(p, p, p, w9, bias)


def _cumsum_op(tb, length, reverse):
    r = lax.broadcasted_iota(jnp.int32, (tb, tb), 0)
    c = lax.broadcasted_iota(jnp.int32, (tb, tb), 1)
    shift = length.bit_length() - 1
    same = (r >> shift) == (c >> shift)
    return jnp.logical_and(same, (c >= r) if reverse else (c <= r)).astype(BF16)


def _apply_op(op, x):
    hi = x.astype(BF16)
    r1 = x - hi.astype(F32)
    mid = r1.astype(BF16)
    lo = (r1 - mid.astype(F32)).astype(BF16)
    n = x.shape[1]
    out = _dot(op, jnp.concatenate([hi, mid, lo], axis=1))
    return out[:, 0:n] + out[:, n:2 * n] + out[:, 2 * n:3 * n]


def _causal(length, reverse):
    r = lax.broadcasted_iota(jnp.int32, (length, length), 0)
    c = lax.broadcasted_iota(jnp.int32, (length, length), 1)
    return (c >= r) if reverse else (c <= r)


def _chunk_order(tb, reverse):
    idx = list(range(tb // CHUNK))
    return idx[::-1] if reverse else idx


LANE = 128


def _half_masks():
    lane_lo = lax.broadcasted_iota(jnp.int32, (1, LANE), 1) < LANE // 2
    sub_lo = lax.broadcasted_iota(jnp.int32, (LANE, 1), 0) < LANE // 2
    return lane_lo, sub_lo


def _ssd_block(xs_ref, bm_ref, cm_ref, sm_ref, bias, aneg, y_ref, row0, s_ref, op, lmask, *, d, reverse, heads, hdim):
    tb = xs_ref.shape[0]
    hpg = heads // SSD_GROUPS
    assert hdim == LANE // 2 and SSD_STATE == LANE // 2 and hpg % 2 == 0 and SSD_GROUPS % 2 == 0
    end = 0 if reverse else tb - 1
    lane_lo, sub_lo = _half_masks()
    dt = _softplus(sm_ref[...] + bias)
    acum = _apply_op(op, dt * aneg)
    acum_t = acum.T
    dt_t = dt.T
    w_t = jnp.exp(acum_t[:, end:end + 1] - acum_t) * dt_t
    e_end = jnp.exp(acum[end:end + 1, :])
    xs = xs_ref[...].astype(BF16)
    bm_t = bm_ref[...].T
    cm = cm_ref[...].astype(BF16)
    for gp in range(SSD_GROUPS // 2):
        cm_pair = cm[:, gp * LANE:(gp + 1) * LANE]
        bm_pair_t = bm_t[gp * LANE:(gp + 1) * LANE, :]
        for gi in range(2):
            g = 2 * gp + gi
            in_g = sub_lo if gi == 0 else jnp.logical_not(sub_lo)
            cb = _dot(cm_pair, jnp.where(in_g, bm_pair_t, 0.0).astype(BF16))
            bmg_t = bm_t[g * SSD_STATE:(g + 1) * SSD_STATE, :]
            s_g = s_ref[g]
            zero = jnp.zeros_like(s_g)
            s_rows = jnp.concatenate([s_g, zero] if gi == 0 else [zero, s_g], axis=0).astype(BF16)
            y_in = _dot(cm_pair, s_rows)
            for pi in range(hpg // 2):
                gl = slice(pi * LANE, (pi + 1) * LANE)
                xl = slice(g * hpg * hdim + pi * LANE, g * hpg * hdim + (pi + 1) * LANE)
                xs_pair = xs[:, xl]
                xs_half = (jnp.where(lane_lo, xs_pair, 0.0).astype(BF16), jnp.where(lane_lo, 0.0, xs_pair).astype(BF16))
                c0 = d * heads + g * hpg + 2 * pi
                y_acc = None
                s_acc = None
                e_col = []
                for hi in range(2):
                    c = c0 + hi
                    colb = jnp.broadcast_to(acum[:, c:c + 1], (tb, tb))
                    m = cb * jnp.exp(jnp.where(lmask, colb - acum_t[c:c + 1, :], -jnp.inf)) * dt_t[c:c + 1, :]
                    y_h = _dot(m.astype(BF16), xs_half[hi])
                    s_h = _dot((bmg_t * w_t[c:c + 1, :]).astype(BF16), xs_half[hi])
                    y_acc = y_h if y_acc is None else y_acc + y_h
                    s_acc = s_h if s_acc is None else s_acc + s_h
                    e_col.append(jnp.exp(colb[:, 0:LANE]))
                y_ref[row0:row0 + tb, xl] = y_acc + y_in[:, gl] * jnp.where(lane_lo, e_col[0], e_col[1])
                e_pair = jnp.where(lane_lo, e_end[:, c0:c0 + 1], e_end[:, c0 + 1:c0 + 2])
                s_ref[g, :, gl] = s_g[:, gl] * e_pair + s_acc


def _ssd_kernel(*refs, batch, d, reverse, heads, hdim):
    ins, (bias_ref, aneg_ref, y_ref, s_scr) = refs[:4 * batch], refs[4 * batch:]

    @pl.when(pl.program_id(0) == 0)
    def _():
        s_scr[...] = jnp.zeros_like(s_scr)

    tb = SCAN_BLOCK
    op = _cumsum_op(tb, tb, reverse)
    lmask = _causal(tb, reverse)
    for b in range(batch):
        _ssd_block(*ins[4 * b:4 * b + 4], bias_ref[...], aneg_ref[...], y_ref, b * tb, s_scr.at[b], op, lmask,
                   d=d, reverse=reverse, heads=heads, hdim=hdim)


def _ml_kernel(*refs, batch, d, reverse, heads, gate_col0):
    ins, (gbi_ref, gbf_ref, y_ref, ct_scr, m_scr) = refs[:5 * batch], refs[5 * batch:]

    @pl.when(pl.program_id(0) == 0)
    def _():
        ct_scr[...] = jnp.zeros_like(ct_scr)
        m_scr[...] = jnp.zeros_like(m_scr)

    tb = SCAN_BLOCK
    op = _cumsum_op(tb, tb, reverse)
    lmask = _causal(tb, reverse)
    states = [([ct_scr[b, h] for h in range(heads)], [m_scr[b, h][:, 0:1] for h in range(heads)]) for b in range(batch)]
    new = [([], []) for _ in range(batch)]
    _interleave(_ml_block(*ins[5 * b:5 * b + 5], gbi_ref[...], gbf_ref[...], y_ref, b * tb, states[b][0], states[b][1],
                          new[b][0], new[b][1], op, lmask, d=d, reverse=reverse, heads=heads, gate_col0=gate_col0)
                for b in range(batch))
    for b in range(batch):
        for h in range(heads):
            ct_scr[b, h] = new[b][0][h]
            m_scr[b, h] = jnp.broadcast_to(new[b][1][h], m_scr.shape[2:])


def _cummax_rows(x, reverse):
    n = x.shape[0]
    row = lax.broadcasted_iota(jnp.int32, (n, 1), 0)
    s = 1
    while s < n:
        if s < 8:
            if reverse:
                shifted = jnp.where(row < n - s, pltpu.roll(x, n - s, 0), -jnp.inf)
            else:
                shifted = jnp.where(row >= s, pltpu.roll(x, s, 0), -jnp.inf)
        else:
            pad = jnp.full((s, x.shape[1]), -jnp.inf, F32)
            shifted = jnp.concatenate([x[s:], pad] if reverse else [pad, x[:n - s]], axis=0)
        x = jnp.maximum(x, shifted)
        s *= 2
    return x


def _ml_block(q_ref, k_ref, v_ref, smi_ref, smf_ref, gate_bi, gate_bf, y_ref, row0, ct_in, m_in, ct_out, m_out, op, lmask,
              *, d, reverse, heads, gate_col0):
    tb = q_ref.shape[0]
    hd = q_ref.shape[1] // heads
    assert hd == LANE
    end = 0 if reverse else tb - 1
    base = gate_col0 + d * 2 * heads
    raw = smi_ref[...] + gate_bi
    bcum = _apply_op(op, _log_sigmoid(smf_ref[...] + gate_bf))
    r = raw - bcum
    r_t = r.T
    cmax = _cummax_rows(r, reverse)
    q_all = (q_ref[...] * (hd ** -0.5)).astype(BF16)
    k_all = k_ref[...].astype(BF16)
    v_all = v_ref[...].astype(BF16)
    ones = jnp.ones((tb, hd), BF16)
    yield
    for h in range(heads):
        c = base + h
        hs = slice(h * hd, (h + 1) * hd)
        qb = q_all[:, hs]
        k_t = k_all[:, hs].T
        v_ext = jnp.concatenate([v_all[:, hs], ones], axis=1)
        m_st = m_in[h]
        ct = ct_in[h]
        r_row = r_t[c:c + 1, :]
        b_rep = jnp.broadcast_to(bcum[:, c:c + 1], (tb, hd))
        u_rep = jnp.broadcast_to(jnp.maximum(cmax[:, c:c + 1], m_st), (tb, hd))
        s = jnp.exp(jnp.where(lmask, r_row - jnp.concatenate([u_rep, u_rep], axis=1), -jnp.inf)) * _dot(qb, k_t)
        g_rep = jnp.exp(m_st - u_rep)
        acc = _dot(s.astype(BF16), v_ext) + jnp.concatenate([g_rep, g_rep], axis=1) * _dot(qb, ct.astype(BF16))
        y_ref[row0:row0 + tb, hs] = acc[:, 0:hd] / jnp.maximum(jnp.abs(acc[:, hd:]), jnp.exp(-(b_rep + u_rep)))
        b_end = bcum[end:end + 1, c:c + 1]
        a_row = b_end + r_row
        m_loc = jnp.max(a_row, axis=1, keepdims=True)
        upd = _dot((k_t * jnp.exp(a_row - m_loc)).astype(BF16), v_ext)
        m_new = jnp.maximum(b_end + m_st, m_loc)
        ct_out.append(jnp.exp(b_end + m_st - m_new) * ct + jnp.exp(m_loc - m_new) * upd)
        m_out.append(m_new)
        yield


def _gla_block(q_ref, k_ref, v_ref, sm_ref, w2, b2, y_ref, row0, s_scr, op, lmask, *, reverse, heads):
    tb = q_ref.shape[0]
    dk = q_ref.shape[1] // heads
    dv = v_ref.shape[1] // heads
    assert dk == LANE // 2 and heads % 2 == 0
    lane_lo, _ = _half_masks()
    x = _dot(sm_ref[...].astype(BF16), w2) + b2
    b = _apply_op(op, _log_sigmoid(x) * (1.0 / GLA_TAU))
    q_d = (q_ref[...] * (dk ** -0.5) * jnp.exp(b)).astype(BF16)
    k = k_ref[...]
    k_d = k * jnp.exp(-b)
    v = v_ref[...].astype(BF16)
    state = [s_scr[h] for h in range(heads)]
    for ci in _chunk_order(tb, reverse):
        lo = ci * CHUNK
        sl = slice(lo, lo + CHUNK)
        end = lo if reverse else lo + CHUNK - 1
        b_end = b[end:end + 1, :]
        kk = k[sl] * jnp.exp(b_end - b[sl])
        e_end = jnp.exp(b_end)
        att, inter = [], []
        for h in range(heads):
            pl_ = slice(h // 2 * LANE, (h // 2 + 1) * LANE)
            mine = lane_lo if h % 2 == 0 else jnp.logical_not(lane_lo)
            vs = slice(h * dv, (h + 1) * dv)
            q_pair = q_d[sl, pl_]
            k_m = jnp.where(mine, k_d[sl, pl_], 0.0).astype(BF16)
            kk_m = jnp.where(mine, kk[:, pl_], 0.0).astype(BF16)
            att.append(_dot_nt(q_pair, k_m))
            inter.append(_dot_nt(q_pair, state[h].astype(BF16)))
            state[h] = state[h] * e_end[:, pl_] + _dot_tn(v[sl, vs], kk_m)
        yield
        outs = [_dot(jnp.where(lmask, att[h], 0.0).astype(BF16), v[sl, h * dv:(h + 1) * dv]) + inter[h]
                for h in range(heads)]
        y_ref[row0 + lo:row0 + lo + CHUNK, :] = jnp.concatenate(outs, axis=1)
        yield
    for h in range(heads):
        s_scr[h] = state[h]


def _interleave(gens):
    gens = list(gens)
    while gens:
        alive = []
        for g in gens:
            try:
                next(g)
                alive.append(g)
            except StopIteration:
                pass
        gens = alive


def _gla_kernel(*refs, batch, reverse, heads):
    ins, (w2_ref, b2_ref, y_ref, s_scr) = refs[:4 * batch], refs[4 * batch:]

    @pl.when(pl.program_id(0) == 0)
    def _():
        s_scr[...] = jnp.zeros_like(s_scr)

    tb = SCAN_BLOCK
    op = _cumsum_op(tb, CHUNK, reverse)
    lmask = _causal(CHUNK, reverse)
    _interleave(_gla_block(*ins[4 * b:4 * b + 4], w2_ref[...], b2_ref[...], y_ref, b * tb, s_scr.at[b], op, lmask,
                           reverse=reverse, heads=heads) for b in range(batch))


def _scan_rows(batch, seq, ctx_len, reverse):
    tb = SCAN_BLOCK
    nc, nl = ctx_len // tb, seq // tb
    ctx_base = batch * seq // tb

    def rows(b, s):
        if reverse:
            return jnp.where(s < nc, ctx_base + b * nc + (nc - 1 - s), b * nl + (nl - 1 - (s - nc)))
        return jnp.where(s < nc, ctx_base + b * nc + s, b * nl + (s - nc))

    return rows, nc + nl


def _scan_out_rows(batch, seq, ctx_len, reverse, tile):
    tb = SCAN_BLOCK
    nc, nl = ctx_len // tb, seq // tb
    sub = tb // tile

    def rows(i):
        q, part = i // sub, i % sub
        is_lat = q < batch * nl
        qc = jnp.maximum(q - batch * nl, 0)
        b = jnp.where(is_lat, q // nl, qc // nc)
        k = jnp.where(is_lat, q % nl, qc % nc)
        n = jnp.where(is_lat, nl, nc)
        s = jnp.where(is_lat, nc, 0) + (n - 1 - k if reverse else k)
        return (s * batch + b) * sub + part

    return rows


def _scan_call(kern, name, batch, rows, steps, ins, consts, out_w, scratch):
    tb = SCAN_BLOCK
    in_specs = [pl.BlockSpec((tb, w), functools.partial(lambda s, b, cb: (rows(b, s), cb), b=b, cb=cb))
                for b in range(batch) for _, w, cb in ins]
    in_specs += [pl.BlockSpec(c.shape, lambda s: (0, 0)) for c in consts]
    return pl.pallas_call(
        kern,
        grid=(steps,),
        in_specs=in_specs,
        out_specs=pl.BlockSpec((batch * tb, out_w), lambda s: (s, 0)),
        out_shape=jax.ShapeDtypeStruct((steps * batch * tb, out_w), F32),
        scratch_shapes=scratch,
        compiler_params=_cparams(("arbitrary",)),
        name=name,
    )(*[a for _ in range(batch) for a, _, _ in ins], *consts)


def _group_norm(y, groups, center):
    gw = y.shape[1] // groups
    outs = []
    for g in range(groups):
        yg = y[:, g * gw:(g + 1) * gw]
        if center:
            yg = yg - jnp.mean(yg, axis=-1, keepdims=True)
        outs.append(yg * lax.rsqrt(jnp.mean(yg * yg, axis=-1, keepdims=True) + EPS))
    return jnp.concatenate(outs, axis=1)


def _merge_kernel(h_ref, mod_ref, gate_ref, z_ref, xs_ref, ysf_ref, ysb_ref, o_ref, ymf_ref, ymb_ref,
                  r_ref, ygf_ref, ygb_ref, dskip_ref, snw_ref, mnw_ref, gnw_ref, mb_ref,
                  wbs_ref, wbm_ref, wbg_ref, wo_ref, lng_ref, lnb_ref, out_ref, *, alpha, ml_heads):
    d = h_ref.shape[1]
    ys = (ysf_ref[...] + ysb_ref[...] + dskip_ref[...] * xs_ref[...]) * _silu(z_ref[...])
    ys = _group_norm(ys, SSD_GROUPS, False) * snw_ref[...]
    b_ssd = _dot(ys.astype(BF16), wbs_ref[...])
    ym = _sigmoid(o_ref[...]) * (ymf_ref[...] + ymb_ref[...])
    ym = _group_norm(ym, ml_heads, True) * mnw_ref[...]
    b_ml = _dot(ym.astype(BF16), wbm_ref[...])
    yg = _group_norm(ygf_ref[...] + ygb_ref[...], GLA_HEADS, True) * gnw_ref[...] * _silu(r_ref[...])
    b_gla = _dot(yg.astype(BF16), wbg_ref[...])
    mb = mb_ref[...]
    mix = (_sigmoid(gate_ref[:, 0:d] + mb[0:1]) * b_ssd
           + _sigmoid(gate_ref[:, d:2 * d] + mb[1:2]) * b_ml
           + _sigmoid(gate_ref[:, 2 * d:3 * d] + mb[2:3]) * b_gla)
    y = _dot(mix.astype(BF16), wo_ref[...])
    m = mod_ref[0]
    out_ref[...] = _layer_norm(alpha * h_ref[...] + m[2:3] * y, lng_ref[...], lnb_ref[...])


def kernel(x, c, ctx, c_ctx, w_mod, b_mod, ln_g, ln_b, ffn_w_in, ffn_w_out, w_in, merge_b, ssd_conv_w, ssd_conv_b, ssd_dt_bias, ssd_a_log, ssd_d, ssd_norm_w, ml_conv_w, ml_conv_b, ml_gate_b, ml_norm_w, gla_w2, gla_b2, gla_norm_w, w_br_ssd, w_br_ml, w_br_gla, w_out):
    batch, seq, d = x.shape
    ctx_len = ctx.shape[1]
    depth = w_mod.shape[0]
    alpha = (2 * depth) ** 0.25
    ssd_inner = ssd_norm_w.shape[1]
    ssd_heads = ssd_dt_bias.shape[2]
    ssd_hdim = ssd_inner // ssd_heads
    bc_w = 2 * SSD_GROUPS * SSD_STATE
    ml_w = ml_norm_w.shape[1]
    ml_heads = ml_gate_b.shape[3]
    gla_w = gla_norm_w.shape[1]
    gla_kw = gla_w2.shape[3]
    gla_rank = gla_w2.shape[2]
    n_lat = batch * seq
    n_all = n_lat + batch * ctx_len
    tb = SCAN_BLOCK
    assert seq % tb == 0 and ctx_len == tb
    assert 2 * ssd_heads + 4 * ml_heads + 2 * gla_rank <= SMALL_W

    tm = next(t for t in (1024, 512, tb) if seq % t == 0 and (batch * ctx_len) % t == 0)
    tm_merge = 128
    tf = 512

    def group_of_tile_for(t):
        per_batch = seq // t
        return lambda i: jnp.minimum(i // per_batch, batch)

    assert (d, ssd_inner, ml_w, gla_w, gla_kw, bc_w) == (2048, 1024, 512, 512, 256, 512)
    gate_w = 3 * d
    col_z = gate_w
    col_mlv = col_z + ssd_inner
    col_mlqk = col_mlv + ml_w
    col_xs = col_mlqk + 2 * ml_w
    col_bc = col_xs + ssd_inner
    col_mlo = col_bc + bc_w
    col_glav = col_mlo + ml_w
    col_glar = col_glav + gla_w
    col_glaq = col_glar + gla_w
    col_glak = col_glaq + gla_kw
    col_small = col_glak + gla_kw
    tn_proj = 1280
    n_proj = -(-(col_small + 2 * SMALL_W) // tn_proj) * tn_proj
    conv_w = 2 * ml_w + ssd_inner + bc_w
    tc_conv = conv_w // 2
    assert col_mlqk % tc_conv == 0

    in_sizes = (ssd_inner, ssd_inner + bc_w, 2 * ssd_heads, ml_w, ml_w, ml_w, ml_w, 4 * ml_heads,
                gla_kw, gla_kw, gla_w, gla_w, 2 * gla_rank, gate_w)
    offs = [0]
    for sz in in_sizes:
        offs.append(offs[-1] + sz)
    (o_z, o_xbc, o_dt, o_mlq, o_mlk, o_mlv, o_mlo, o_mlg, o_gq, o_gk, o_gv, o_gr, o_lr, o_gate, o_end) = offs

    rows_mod = 8
    cpad = jnp.concatenate([c, c_ctx[None], jnp.zeros((rows_mod - batch - 1, d), F32)], axis=0)
    mod = _modulation(cpad, w_mod, b_mod).reshape(depth, rows_mod, N_MOD, d)

    ffn_in_bf = ffn_w_in.astype(BF16)
    ffn_out_bf = ffn_w_out.astype(BF16)
    w_br_bf = [w.astype(BF16) for w in (w_br_ssd, w_br_ml, w_br_gla)]
    w_out_bf = w_out.astype(BF16)

    h = None
    for l in range(depth):
        last = l == depth - 1
        n_out = n_lat if last else n_all

        wl = w_in[l].astype(BF16)
        g0 = 2 * ssd_heads
        small_used = 2 * ssd_heads + 4 * ml_heads + 2 * gla_rank
        fgate_cols = []
        for dd in range(2):
            f0 = o_mlg + (2 * dd + 1) * ml_heads
            fgate_cols += [wl[:, f0:f0 + ml_heads], jnp.zeros((d, ml_heads), BF16)]
        w_big = jnp.concatenate(
            [wl[:, o_gate:o_end], wl[:, o_z:o_xbc], wl[:, o_mlv:o_mlo], wl[:, o_mlq:o_mlv], wl[:, o_xbc:o_dt],
             wl[:, o_mlo:o_mlg], wl[:, o_gv:o_lr], wl[:, o_gq:o_gv],
             wl[:, o_dt:o_mlq], wl[:, o_mlg:o_gq], wl[:, o_lr:o_gate], jnp.zeros((d, SMALL_W - small_used), BF16),
             jnp.zeros((d, g0), BF16)] + fgate_cols
            + [jnp.zeros((d, n_proj - col_small - SMALL_W - g0 - 4 * ml_heads), BF16)], axis=1)
        conv_w9 = jnp.concatenate([ml_conv_w[l].reshape(9, 2 * ml_w), ssd_conv_w[l].reshape(9, ssd_inner + bc_w)], axis=1)
        conv_b = jnp.concatenate([ml_conv_b[l], ssd_conv_b[l]])[None]

        def small_row(vals, start):
            return jnp.zeros((1, SMALL_W), F32).at[0, start:start + vals.shape[0]].set(vals)

        dt_bias_row = small_row(ssd_dt_bias[l].reshape(-1), 0)
        a_neg_row = small_row(-jnp.exp(ssd_a_log[l].astype(F32)).reshape(-1), 0)
        gate_bi_row = small_row(ml_gate_b[l].reshape(-1), g0)
        gate_bf_row = small_row(jnp.concatenate([ml_gate_b[l, :, 1], jnp.zeros((2, ml_heads), F32)], axis=1).reshape(-1), g0)
        lr0 = 2 * ssd_heads + 4 * ml_heads
        w2_pad = [jnp.zeros((SMALL_W, gla_kw), F32).at[lr0 + dd * gla_rank:lr0 + (dd + 1) * gla_rank].set(gla_w2[l, dd])
                  for dd in range(2)]

        ffn0 = functools.partial(_ffn, mod3=mod[l, :, 0:3], group_of_tile=group_of_tile_for(tm), w_in_bf=ffn_in_bf,
                                 w_out_bf=ffn_out_bf, lk=(l, 0), ln_g=ln_g[l, 0], ln_b=ln_b[l, 0], alpha=alpha,
                                 tm=tm, tf=tf)
        if h is None:
            h = ffn0(x.reshape(n_lat, d), n_lat, out_rows=n_all)
            h = ffn0(ctx.reshape(batch * ctx_len, d), batch * ctx_len, out_rows=n_all, tile0=n_lat // tm, out_buf=h)
        else:
            h = ffn0(h, n_all)

        p = _proj(h, mod[l, :, 3:6], group_of_tile_for(tm), w_big, tm, tn_proj)
        cv = _conv(p, col_mlqk // tc_conv, conv_w9, conv_b, n_lat, seq, tc_conv)
        col_cvb = 2 * ml_w + ssd_inner
        ys, ym, yg = [], [], []
        for dd, reverse in ((0, False), (1, True)):
            rows, steps = _scan_rows(batch, seq, ctx_len, reverse)
            ys.append(_scan_call(
                functools.partial(_ssd_kernel, batch=batch, d=dd, reverse=reverse, heads=ssd_heads, hdim=ssd_hdim),
                "ssd_scan", batch, rows, steps,
                [(cv, ssd_inner, 2 * ml_w // ssd_inner), (cv, bc_w // 2, col_cvb // (bc_w // 2)),
                 (cv, bc_w // 2, col_cvb // (bc_w // 2) + 1), (p, SMALL_W, col_small // SMALL_W)],
                [dt_bias_row, a_neg_row], ssd_inner,
                [pltpu.VMEM((batch, SSD_GROUPS, SSD_STATE, ssd_inner // SSD_GROUPS), F32)]))
            ym.append(_scan_call(
                functools.partial(_ml_kernel, batch=batch, d=dd, reverse=reverse, heads=ml_heads,
                                  gate_col0=2 * ssd_heads),
                "mlstm_scan", batch, rows, steps,
                [(cv, ml_w, 0), (cv, ml_w, 1), (p, ml_w, col_mlv // ml_w), (p, SMALL_W, col_small // SMALL_W),
                 (p, SMALL_W, col_small // SMALL_W + 1)],
                [gate_bi_row, gate_bf_row], ml_w,
                [pltpu.VMEM((batch, ml_heads, ml_w // ml_heads, 2 * ml_w // ml_heads), F32),
                 pltpu.VMEM((batch, ml_heads, 1, ml_w // ml_heads), F32)]))
            yg.append(_scan_call(
                functools.partial(_gla_kernel, batch=batch, reverse=reverse, heads=GLA_HEADS),
                "gla_scan", batch, rows, steps,
                [(p, gla_kw, col_glaq // gla_kw), (p, gla_kw, col_glak // gla_kw), (p, gla_w, col_glav // gla_w),
                 (p, SMALL_W, col_small // SMALL_W)],
                [w2_pad[dd].astype(BF16), gla_b2[l, dd][None]], gla_w,
                [pltpu.VMEM((batch, GLA_HEADS, gla_w // GLA_HEADS, LANE), F32)]))

        tmm = tm_merge
        row = lambda i: (i, 0)
        const = lambda i: (0, 0)
        y_fwd = _scan_out_rows(batch, seq, ctx_len, False, tmm)
        y_bwd = _scan_out_rows(batch, seq, ctx_len, True, tmm)
        row_f = lambda i: (y_fwd(i), 0)
        row_b = lambda i: (y_bwd(i), 0)

        def col(cb):
            return lambda i: (i, cb)

        g_of = group_of_tile_for(tmm)
        consts = [jnp.repeat(ssd_d[l], ssd_hdim)[None], ssd_norm_w[l][None], ml_norm_w[l][None], gla_norm_w[l][None],
                  merge_b[l]]
        weights = [w_br_bf[0], w_br_bf[1], w_br_bf[2], w_out_bf]
        ln_consts = [ln_g[l, 1][None], ln_b[l, 1][None]]
        h_mix = pl.pallas_call(
            functools.partial(_merge_kernel, alpha=alpha, ml_heads=ml_heads),
            grid=(n_out // tmm,),
            in_specs=[pl.BlockSpec((tmm, d), row),
                      pl.BlockSpec((1, 3, d), lambda i: (g_of(i), 0, 0)),
                      pl.BlockSpec((tmm, gate_w), col(0)),
                      pl.BlockSpec((tmm, ssd_inner), col(col_z // ssd_inner)),
                      pl.BlockSpec((tmm, ssd_inner), col(2 * ml_w // ssd_inner)),
                      pl.BlockSpec((tmm, ssd_inner), row_f),
                      pl.BlockSpec((tmm, ssd_inner), row_b),
                      pl.BlockSpec((tmm, ml_w), col(col_mlo // ml_w)),
                      pl.BlockSpec((tmm, ml_w), row_f),
                      pl.BlockSpec((tmm, ml_w), row_b),
                      pl.BlockSpec((tmm, gla_w), col(col_glar // gla_w)),
                      pl.BlockSpec((tmm, gla_w), row_f),
                      pl.BlockSpec((tmm, gla_w), row_b)]
                     + [pl.BlockSpec(a.shape, const) for a in consts]
                     + [pl.BlockSpec((None,) + a.shape[1:], functools.partial(lambda i, l: (l, 0, 0), l=l))
                        for a in weights]
                     + [pl.BlockSpec(a.shape, const) for a in ln_consts],
            out_specs=pl.BlockSpec((tmm, d), row),
            out_shape=jax.ShapeDtypeStruct((n_out, d), F32),
            compiler_params=_cparams(("arbitrary",)),
            name="merge_out",
        )(h, mod[l, :, 3:6], p, p, cv, ys[0], ys[1], p, ym[0], ym[1], p, yg[0], yg[1], *consts, *weights, *ln_consts)

        h = _ffn(h_mix, n_out, mod[l, :, 6:9], group_of_tile_for(tm), ffn_in_bf, ffn_out_bf, (l, 1), ln_g[l, 2],
                 ln_b[l, 2], alpha, tm, tf)

    return h.reshape(batch, seq, d)
```

```python
import functools

import jax
import jax.numpy as jnp
from jax import lax
from jax.experimental import pallas as pl
from jax.experimental.pallas import tpu as pltpu

F32 = jnp.float32
BF16 = jnp.bfloat16
HIGHEST = lax.Precision.HIGHEST

GRID_W = 64
CHUNK = 64
SSD_GROUPS = 4
SSD_STATE = 64
GLA_HEADS = 4
GLA_TAU = 16.0
EPS = 1e-5
N_MOD = 9

SCAN_BLOCK = 256
SMALL_W = 128
V7X_VMEM_LIMIT = 56 * 1024 * 1024
V7X_VMEM_LIMIT_BIG = 60 * 1024 * 1024

NT_DIMS = (((1,), (1,)), ((), ()))
TN_DIMS = (((0,), (0,)), ((), ()))


def _cparams(sem, vmem=V7X_VMEM_LIMIT):
    return pltpu.CompilerParams(dimension_semantics=sem, vmem_limit_bytes=vmem)


def _dot(a, b):
    return jnp.dot(a, b, preferred_element_type=F32)


def _dot_exact(a, b):
    return jnp.dot(a, b, preferred_element_type=F32, precision=HIGHEST)


def _dot_nt(a, b):
    return lax.dot_general(a, b, NT_DIMS, preferred_element_type=F32)


def _dot_tn(a, b):
    return lax.dot_general(a, b, TN_DIMS, preferred_element_type=F32)


def _sigmoid(x):
    return 1.0 / (1.0 + jnp.exp(-x))


def _silu(x):
    return x * _sigmoid(x)


def _softplus(x):
    return jnp.maximum(x, 0.0) + jnp.log(1.0 + jnp.exp(-jnp.abs(x)))


def _log_sigmoid(x):
    return -_softplus(-x)


def _layer_norm(x, g, b):
    mu = jnp.mean(x, axis=-1, keepdims=True)
    xc = x - mu
    var = jnp.mean(xc * xc, axis=-1, keepdims=True)
    return xc * lax.rsqrt(var + EPS) * g + b


def _mod_kernel(c_ref, w_ref, b_ref, o_ref):
    c = c_ref[...]
    o_ref[0] = _dot_exact(_silu(c), w_ref[0]) + b_ref[0]


def _modulation(cpad, w_mod, b_mod, tn=1152):
    depth, d, n = w_mod.shape
    rows = cpad.shape[0]
    return pl.pallas_call(
        _mod_kernel,
        grid=(depth, n // tn),
        in_specs=[pl.BlockSpec((rows, d), lambda l, j: (0, 0)),
                  pl.BlockSpec((1, d, tn), lambda l, j: (l, 0, j)),
                  pl.BlockSpec((1, 1, tn), lambda l, j: (l, 0, j))],
        out_specs=pl.BlockSpec((1, rows, tn), lambda l, j: (l, 0, j)),
        out_shape=jax.ShapeDtypeStruct((depth, rows, n), F32),
        compiler_params=_cparams(("arbitrary", "arbitrary")),
        name="modulation",
    )(cpad, w_mod, b_mod.reshape(depth, 1, n))


def _ffn_kernel(h_ref, mod_ref, wa_ref, wg_ref, wo_ref, lng_ref, lnb_ref, o_ref, *, alpha):
    j = pl.program_id(1)
    m = mod_ref[0]
    u = (h_ref[...] * (1.0 + m[1:2]) + m[0:1]).astype(BF16)
    a = _dot(u, wa_ref[...])
    g = _dot(u, wg_ref[...])

    @pl.when(j == 0)
    def _():
        o_ref[...] = jnp.zeros_like(o_ref)

    o_ref[...] += _dot((_silu(g) * a).astype(BF16), wo_ref[...])

    @pl.when(j == pl.num_programs(1) - 1)
    def _():
        x = alpha * h_ref[...] + m[2:3] * (0.5 * o_ref[...])
        o_ref[...] = _layer_norm(x, lng_ref[...], lnb_ref[...])


def _ffn_kernel_inplace(h_ref, mod_ref, wa_ref, wg_ref, wo_ref, lng_ref, lnb_ref, buf_ref, o_ref, *, alpha):
    del buf_ref
    _ffn_kernel(h_ref, mod_ref, wa_ref, wg_ref, wo_ref, lng_ref, lnb_ref, o_ref, alpha=alpha)


def _ffn(h, n_rows, mod3, group_of_tile, w_in_bf, w_out_bf, lk, ln_g, ln_b, alpha, tm, tf, out_rows=None, tile0=0,
         out_buf=None):
    d = h.shape[1]
    dff = w_out_bf.shape[2]
    nf = dff // tf
    l, k = lk
    out_rows = n_rows if out_rows is None else out_rows
    in_specs = [pl.BlockSpec((tm, d), lambda i, j: (i, 0)),
                pl.BlockSpec((1, 3, d), lambda i, j: (group_of_tile(i + tile0), 0, 0)),
                pl.BlockSpec((None, None, d, tf), lambda i, j: (l, k, 0, j)),
                pl.BlockSpec((None, None, d, tf), lambda i, j: (l, k, 0, j + nf)),
                pl.BlockSpec((None, None, tf, d), lambda i, j: (l, k, j, 0)),
                pl.BlockSpec((1, d), lambda i, j: (0, 0)),
                pl.BlockSpec((1, d), lambda i, j: (0, 0))]
    args = [h, mod3, w_in_bf, w_in_bf, w_out_bf, ln_g.reshape(1, d), ln_b.reshape(1, d)]
    kern = functools.partial(_ffn_kernel, alpha=alpha)
    aliases = {}
    if out_buf is not None:
        in_specs.append(pl.BlockSpec(memory_space=pl.ANY))
        args.append(out_buf)
        aliases = {len(args) - 1: 0}
        kern = functools.partial(_ffn_kernel_inplace, alpha=alpha)
    return pl.pallas_call(
        kern,
        grid=(n_rows // tm, nf),
        in_specs=in_specs,
        out_specs=pl.BlockSpec((tm, d), lambda i, j: (i + tile0, 0)),
        out_shape=jax.ShapeDtypeStruct((out_rows, d), F32),
        input_output_aliases=aliases,
        compiler_params=_cparams(("arbitrary", "arbitrary"), V7X_VMEM_LIMIT_BIG),
        name="ffn",
    )(*args)


def _permute_cast_kernel(w_ref, o_ref, *, segments):
    x = w_ref[0]
    parts = [jnp.zeros((x.shape[0], seg), F32) if isinstance(seg, int) else x[:, seg[0]:seg[1]] for seg in segments]
    o_ref[0] = jnp.concatenate(parts, axis=1).astype(BF16)


def _permute_cast(w, segments, tr=128):
    depth, rows, cols = w.shape
    n_out = sum(seg if isinstance(seg, int) else seg[1] - seg[0] for seg in segments)
    return pl.pallas_call(
        functools.partial(_permute_cast_kernel, segments=tuple(segments)),
        grid=(depth, rows // tr),
        in_specs=[pl.BlockSpec((1, tr, cols), lambda l, i: (l, i, 0))],
        out_specs=pl.BlockSpec((1, tr, n_out), lambda l, i: (l, i, 0)),
        out_shape=jax.ShapeDtypeStruct((depth, rows, n_out), BF16),
        compiler_params=_cparams(("arbitrary", "arbitrary")),
        name="permute_cast",
    )(w)


def _proj_kernel(h_ref, mod_ref, w_ref, o_ref, u_scr):
    @pl.when(pl.program_id(1) == 0)
    def _():
        m = mod_ref[0]
        u_scr[...] = (h_ref[...] * (1.0 + m[1:2]) + m[0:1]).astype(BF16)

    o_ref[...] = _dot(u_scr[...], w_ref[...])


def _proj(h, mod3, group_of_tile, w_big, l, tm, tn):
    n_rows, d = h.shape
    n = w_big.shape[2]
    return pl.pallas_call(
        _proj_kernel,
        grid=(n_rows // tm, n // tn),
        in_specs=[pl.BlockSpec((tm, d), lambda i, j: (i, 0)),
                  pl.BlockSpec((1, 3, d), lambda i, j: (group_of_tile(i), 0, 0)),
                  pl.BlockSpec((None, d, tn), lambda i, j: (l, 0, j))],
        out_specs=pl.BlockSpec((tm, tn), lambda i, j: (i, j)),
        out_shape=jax.ShapeDtypeStruct((n_rows, n), F32),
        scratch_shapes=[pltpu.VMEM((tm, d), BF16)],
        compiler_params=_cparams(("arbitrary", "arbitrary")),
        name="in_proj",
    )(h, mod3, w_big)


def _conv_kernel(c_ref, p_ref, n_ref, w_ref, b_ref, o_ref, *, n_lat_blocks, blocks_per_seq):
    i = pl.program_id(0)
    tb = c_ref.shape[0]
    is_lat = i < n_lat_blocks
    in_seq = i % blocks_per_seq
    keep_prev = jnp.logical_and(is_lat, in_seq != 0)
    keep_next = jnp.logical_and(is_lat, in_seq != blocks_per_seq - 1)
    cur = c_ref[...]
    up = jnp.concatenate([jnp.where(keep_prev, p_ref[...], 0.0), cur[:tb - GRID_W]], axis=0)
    down = jnp.concatenate([cur[GRID_W:], jnp.where(keep_next, n_ref[...], 0.0)], axis=0)
    w = w_ref[...]
    vert = jnp.where(is_lat, 1.0, 0.0)
    w_up, w_mid, w_dn = w[0:3] * vert, w[3:6], w[6:9] * vert

    def column(dw):
        return w_up[dw:dw + 1] * up + w_mid[dw:dw + 1] * cur + w_dn[dw:dw + 1] * down

    r = lax.broadcasted_iota(jnp.int32, (tb, 1), 0)
    row_len = jnp.where(is_lat, GRID_W, tb)
    pos = r & (row_len - 1)
    acc = (column(1) + b_ref[...]
           + jnp.where(pos != 0, pltpu.roll(column(0), 1, 0), 0.0)
           + jnp.where(pos != row_len - 1, pltpu.roll(column(2), tb - 1, 0), 0.0))
    o_ref[...] = _silu(acc)


def _conv(p, col0_blocks, w9, bias, n_lat_rows, seq, tc):
    n_rows = p.shape[0]
    ch = w9.shape[1]
    tb = SCAN_BLOCK
    halo_per_blk = tb // GRID_W
    n_halo_rows = n_rows // GRID_W
    kern = functools.partial(_conv_kernel, n_lat_blocks=n_lat_rows // tb, blocks_per_seq=seq // tb)
    return pl.pallas_call(
        kern,
        grid=(n_rows // tb, ch // tc),
        in_specs=[pl.BlockSpec((tb, tc), lambda i, j: (i, col0_blocks + j)),
                  pl.BlockSpec((GRID_W, tc), lambda i, j: (jnp.maximum(i * halo_per_blk - 1, 0), col0_blocks + j)),
                  pl.BlockSpec((GRID_W, tc),
                               lambda i, j: (jnp.minimum((i + 1) * halo_per_blk, n_halo_rows - 1), col0_blocks + j)),
                  pl.BlockSpec((9, tc), lambda i, j: (0, j)),
                  pl.BlockSpec((1, tc), lambda i, j: (0, j))],
        out_specs=pl.BlockSpec((tb, tc), lambda i, j: (i, j)),
        out_shape=jax.ShapeDtypeStruct((n_rows, ch), F32),
        compiler_params=_cparams(("arbitrary", "arbitrary")),
        name="short_conv",
    )(p, p, p, w9, bias)


def _cumsum_op(tb, length, reverse):
    r = lax.broadcasted_iota(jnp.int32, (tb, tb), 0)
    c = lax.broadcasted_iota(jnp.int32, (tb, tb), 1)
    shift = length.bit_length() - 1
    same = (r >> shift) == (c >> shift)
    return jnp.logical_and(same, (c >= r) if reverse else (c <= r)).astype(BF16)


def _apply_op(op, x):
    hi = x.astype(BF16)
    r1 = x - hi.astype(F32)
    mid = r1.astype(BF16)
    lo = (r1 - mid.astype(F32)).astype(BF16)
    n = x.shape[1]
    out = _dot(op, jnp.concatenate([hi, mid, lo], axis=1))
    return out[:, 0:n] + out[:, n:2 * n] + out[:, 2 * n:3 * n]


def _causal(length, reverse):
    r = lax.broadcasted_iota(jnp.int32, (length, length), 0)
    c = lax.broadcasted_iota(jnp.int32, (length, length), 1)
    return (c >= r) if reverse else (c <= r)


def _chunk_order(tb, reverse):
    idx = list(range(tb // CHUNK))
    return idx[::-1] if reverse else idx


LANE = 128


def _half_masks():
    lane_lo = lax.broadcasted_iota(jnp.int32, (1, LANE), 1) < LANE // 2
    sub_lo = lax.broadcasted_iota(jnp.int32, (LANE, 1), 0) < LANE // 2
    return lane_lo, sub_lo


def _ssd_block(xs_ref, bm_ref, cm_ref, sm_ref, bias, aneg, y_ref, row0, s_ref, op, lmask, *, d, reverse, heads, hdim):
    tb = xs_ref.shape[0]
    hpg = heads // SSD_GROUPS
    assert hdim == LANE // 2 and SSD_STATE == LANE // 2 and hpg % 2 == 0 and SSD_GROUPS % 2 == 0
    end = 0 if reverse else tb - 1
    lane_lo, sub_lo = _half_masks()
    dt = _softplus(sm_ref[...] + bias)
    acum = _apply_op(op, dt * aneg)
    acum_t = acum.T
    dt_t = dt.T
    w_t = jnp.exp(acum_t[:, end:end + 1] - acum_t) * dt_t
    e_end = jnp.exp(acum[end:end + 1, :])
    xs = xs_ref[...].astype(BF16)
    bm_t = bm_ref[...].T
    cm = cm_ref[...].astype(BF16)
    for gp in range(SSD_GROUPS // 2):
        cm_pair = cm[:, gp * LANE:(gp + 1) * LANE]
        bm_pair_t = bm_t[gp * LANE:(gp + 1) * LANE, :]
        for gi in range(2):
            g = 2 * gp + gi
            in_g = sub_lo if gi == 0 else jnp.logical_not(sub_lo)
            cb = _dot(cm_pair, jnp.where(in_g, bm_pair_t, 0.0).astype(BF16))
            bmg_t = bm_t[g * SSD_STATE:(g + 1) * SSD_STATE, :]
            s_g = s_ref[g]
            zero = jnp.zeros_like(s_g)
            s_rows = jnp.concatenate([s_g, zero] if gi == 0 else [zero, s_g], axis=0).astype(BF16)
            y_in = _dot(cm_pair, s_rows)
            for pi in range(hpg // 2):
                gl = slice(pi * LANE, (pi + 1) * LANE)
                xl = slice(g * hpg * hdim + pi * LANE, g * hpg * hdim + (pi + 1) * LANE)
                xs_pair = xs[:, xl]
                xs_half = (jnp.where(lane_lo, xs_pair, 0.0).astype(BF16), jnp.where(lane_lo, 0.0, xs_pair).astype(BF16))
                c0 = d * heads + g * hpg + 2 * pi
                y_acc = None
                s_acc = None
                e_col = []
                for hi in range(2):
                    c = c0 + hi
                    colb = jnp.broadcast_to(acum[:, c:c + 1], (tb, tb))
                    m = cb * jnp.exp(jnp.where(lmask, colb - acum_t[c:c + 1, :], -jnp.inf)) * dt_t[c:c + 1, :]
                    y_h = _dot(m.astype(BF16), xs_half[hi])
                    s_h = _dot((bmg_t * w_t[c:c + 1, :]).astype(BF16), xs_half[hi])
                    y_acc = y_h if y_acc is None else y_acc + y_h
                    s_acc = s_h if s_acc is None else s_acc + s_h
                    e_col.append(jnp.exp(colb[:, 0:LANE]))
                y_ref[row0:row0 + tb, xl] = y_acc + y_in[:, gl] * jnp.where(lane_lo, e_col[0], e_col[1])
                e_pair = jnp.where(lane_lo, e_end[:, c0:c0 + 1], e_end[:, c0 + 1:c0 + 2])
                s_ref[g, :, gl] = s_g[:, gl] * e_pair + s_acc


def _ssd_kernel(*refs, batch, d, reverse, heads, hdim):
    ins, (bias_ref, aneg_ref, y_ref, s_scr) = refs[:4 * batch], refs[4 * batch:]

    @pl.when(pl.program_id(0) == 0)
    def _():
        s_scr[...] = jnp.zeros_like(s_scr)

    tb = SCAN_BLOCK
    op = _cumsum_op(tb, tb, reverse)
    lmask = _causal(tb, reverse)
    for b in range(batch):
        _ssd_block(*ins[4 * b:4 * b + 4], bias_ref[...], aneg_ref[...], y_ref, b * tb, s_scr.at[b], op, lmask,
                   d=d, reverse=reverse, heads=heads, hdim=hdim)


def _ml_kernel(*refs, batch, d, reverse, heads, gate_col0):
    ins, (gbi_ref, gbf_ref, y_ref, ct_scr, m_scr) = refs[:5 * batch], refs[5 * batch:]

    @pl.when(pl.program_id(0) == 0)
    def _():
        ct_scr[...] = jnp.zeros_like(ct_scr)
        m_scr[...] = jnp.zeros_like(m_scr)

    tb = SCAN_BLOCK
    op = _cumsum_op(tb, tb, reverse)
    lmask = _causal(tb, reverse)
    states = [([ct_scr[b, h] for h in range(heads)], [m_scr[b, h][:, 0:1] for h in range(heads)]) for b in range(batch)]
    new = [([], []) for _ in range(batch)]
    _interleave(_ml_block(*ins[5 * b:5 * b + 5], gbi_ref[...], gbf_ref[...], y_ref, b * tb, states[b][0], states[b][1],
                          new[b][0], new[b][1], op, lmask, d=d, reverse=reverse, heads=heads, gate_col0=gate_col0)
                for b in range(batch))
    for b in range(batch):
        for h in range(heads):
            ct_scr[b, h] = new[b][0][h]
            m_scr[b, h] = jnp.broadcast_to(new[b][1][h], m_scr.shape[2:])


def _cummax_rows(x, reverse):
    n = x.shape[0]
    row = lax.broadcasted_iota(jnp.int32, (n, 1), 0)
    s = 1
    while s < n:
        if s < 8:
            if reverse:
                shifted = jnp.where(row < n - s, pltpu.roll(x, n - s, 0), -jnp.inf)
            else:
                shifted = jnp.where(row >= s, pltpu.roll(x, s, 0), -jnp.inf)
        else:
            pad = jnp.full((s, x.shape[1]), -jnp.inf, F32)
            shifted = jnp.concatenate([x[s:], pad] if reverse else [pad, x[:n - s]], axis=0)
        x = jnp.maximum(x, shifted)
        s *= 2
    return x


def _ml_block(q_ref, k_ref, v_ref, smi_ref, smf_ref, gate_bi, gate_bf, y_ref, row0, ct_in, m_in, ct_out, m_out, op, lmask,
              *, d, reverse, heads, gate_col0):
    tb = q_ref.shape[0]
    hd = q_ref.shape[1] // heads
    assert hd == LANE
    end = 0 if reverse else tb - 1
    base = gate_col0 + d * 2 * heads
    raw = smi_ref[...] + gate_bi
    bcum = _apply_op(op, _log_sigmoid(smf_ref[...] + gate_bf))
    r = raw - bcum
    r_t = r.T
    cmax = _cummax_rows(r, reverse)
    q_all = (q_ref[...] * (hd ** -0.5)).astype(BF16)
    k_all = k_ref[...].astype(BF16)
    v_all = v_ref[...].astype(BF16)
    ones = jnp.ones((tb, hd), BF16)
    yield
    for h in range(heads):
        c = base + h
        hs = slice(h * hd, (h + 1) * hd)
        qb = q_all[:, hs]
        k_t = k_all[:, hs].T
        v_ext = jnp.concatenate([v_all[:, hs], ones], axis=1)
        m_st = m_in[h]
        ct = ct_in[h]
        r_row = r_t[c:c + 1, :]
        b_rep = jnp.broadcast_to(bcum[:, c:c + 1], (tb, hd))
        u_rep = jnp.broadcast_to(jnp.maximum(cmax[:, c:c + 1], m_st), (tb, hd))
        s = jnp.exp(jnp.where(lmask, r_row - jnp.concatenate([u_rep, u_rep], axis=1), -jnp.inf)) * _dot(qb, k_t)
        g_rep = jnp.exp(m_st - u_rep)
        acc = _dot(s.astype(BF16), v_ext) + jnp.concatenate([g_rep, g_rep], axis=1) * _dot(qb, ct.astype(BF16))
        y_ref[row0:row0 + tb, hs] = acc[:, 0:hd] / jnp.maximum(jnp.abs(acc[:, hd:]), jnp.exp(-(b_rep + u_rep)))
        b_end = bcum[end:end + 1, c:c + 1]
        a_row = b_end + r_row
        m_loc = jnp.max(a_row, axis=1, keepdims=True)
        upd = _dot((k_t * jnp.exp(a_row - m_loc)).astype(BF16), v_ext)
        m_new = jnp.maximum(b_end + m_st, m_loc)
        ct_out.append(jnp.exp(b_end + m_st - m_new) * ct + jnp.exp(m_loc - m_new) * upd)
        m_out.append(m_new)
        yield


def _gla_block(q_ref, k_ref, v_ref, sm_ref, w2, b2, y_ref, row0, s_scr, op, lmask, *, reverse, heads):
    tb = q_ref.shape[0]
    dk = q_ref.shape[1] // heads
    dv = v_ref.shape[1] // heads
    assert dk == LANE // 2 and heads % 2 == 0
    lane_lo, _ = _half_masks()
    x = _dot(sm_ref[...].astype(BF16), w2) + b2
    b = _apply_op(op, _log_sigmoid(x) * (1.0 / GLA_TAU))
    q_d = (q_ref[...] * (dk ** -0.5) * jnp.exp(b)).astype(BF16)
    k = k_ref[...]
    k_d = k * jnp.exp(-b)
    v = v_ref[...].astype(BF16)
    state = [s_scr[h] for h in range(heads)]
    for ci in _chunk_order(tb, reverse):
        lo = ci * CHUNK
        sl = slice(lo, lo + CHUNK)
        end = lo if reverse else lo + CHUNK - 1
        b_end = b[end:end + 1, :]
        kk = k[sl] * jnp.exp(b_end - b[sl])
        e_end = jnp.exp(b_end)
        att, inter = [], []
        for h in range(heads):
            pl_ = slice(h // 2 * LANE, (h // 2 + 1) * LANE)
            mine = lane_lo if h % 2 == 0 else jnp.logical_not(lane_lo)
            vs = slice(h * dv, (h + 1) * dv)
            q_pair = q_d[sl, pl_]
            k_m = jnp.where(mine, k_d[sl, pl_], 0.0).astype(BF16)
            kk_m = jnp.where(mine, kk[:, pl_], 0.0).astype(BF16)
            att.append(_dot_nt(q_pair, k_m))
            inter.append(_dot_nt(q_pair, state[h].astype(BF16)))
            state[h] = state[h] * e_end[:, pl_] + _dot_tn(v[sl, vs], kk_m)
        yield
        outs = [_dot(jnp.where(lmask, att[h], 0.0).astype(BF16), v[sl, h * dv:(h + 1) * dv]) + inter[h]
                for h in range(heads)]
        y_ref[row0 + lo:row0 + lo + CHUNK, :] = jnp.concatenate(outs, axis=1)
        yield
    for h in range(heads):
        s_scr[h] = state[h]


def _interleave(gens):
    gens = list(gens)
    while gens:
        alive = []
        for g in gens:
            try:
                next(g)
                alive.append(g)
            except StopIteration:
                pass
        gens = alive


def _gla_kernel(*refs, batch, reverse, heads):
    ins, (w2_ref, b2_ref, y_ref, s_scr) = refs[:4 * batch], refs[4 * batch:]

    @pl.when(pl.program_id(0) == 0)
    def _():
        s_scr[...] = jnp.zeros_like(s_scr)

    tb = SCAN_BLOCK
    op = _cumsum_op(tb, CHUNK, reverse)
    lmask = _causal(CHUNK, reverse)
    _interleave(_gla_block(*ins[4 * b:4 * b + 4], w2_ref[...], b2_ref[...], y_ref, b * tb, s_scr.at[b], op, lmask,
                           reverse=reverse, heads=heads) for b in range(batch))


def _scan_rows(batch, seq, ctx_len, reverse):
    tb = SCAN_BLOCK
    nc, nl = ctx_len // tb, seq // tb
    ctx_base = batch * seq // tb

    def rows(b, s):
        if reverse:
            return jnp.where(s < nc, ctx_base + b * nc + (nc - 1 - s), b * nl + (nl - 1 - (s - nc)))
        return jnp.where(s < nc, ctx_base + b * nc + s, b * nl + (s - nc))

    return rows, nc + nl


def _scan_out_rows(batch, seq, ctx_len, reverse, tile):
    tb = SCAN_BLOCK
    nc, nl = ctx_len // tb, seq // tb
    sub = tb // tile

    def rows(i):
        q, part = i // sub, i % sub
        is_lat = q < batch * nl
        qc = jnp.maximum(q - batch * nl, 0)
        b = jnp.where(is_lat, q // nl, qc // nc)
        k = jnp.where(is_lat, q % nl, qc % nc)
        n = jnp.where(is_lat, nl, nc)
        s = jnp.where(is_lat, nc, 0) + (n - 1 - k if reverse else k)
        return (s * batch + b) * sub + part

    return rows


def _scan_call(kern, name, batch, rows, steps, ins, consts, out_w, scratch):
    tb = SCAN_BLOCK
    in_specs = [pl.BlockSpec((tb, w), functools.partial(lambda s, b, cb: (rows(b, s), cb), b=b, cb=cb))
                for b in range(batch) for _, w, cb in ins]
    in_specs += [pl.BlockSpec(c.shape, lambda s: (0, 0)) for c in consts]
    return pl.pallas_call(
        kern,
        grid=(steps,),
        in_specs=in_specs,
        out_specs=pl.BlockSpec((batch * tb, out_w), lambda s: (s, 0)),
        out_shape=jax.ShapeDtypeStruct((steps * batch * tb, out_w), F32),
        scratch_shapes=scratch,
        compiler_params=_cparams(("arbitrary",)),
        name=name,
    )(*[a for _ in range(batch) for a, _, _ in ins], *consts)


def _group_norm(y, groups, center):
    gw = y.shape[1] // groups
    outs = []
    for g in range(groups):
        yg = y[:, g * gw:(g + 1) * gw]
        if center:
            yg = yg - jnp.mean(yg, axis=-1, keepdims=True)
        outs.append(yg * lax.rsqrt(jnp.mean(yg * yg, axis=-1, keepdims=True) + EPS))
    return jnp.concatenate(outs, axis=1)


def _merge_kernel(h_ref, mod_ref, gate_ref, z_ref, xs_ref, ysf_ref, ysb_ref, o_ref, ymf_ref, ymb_ref,
                  r_ref, ygf_ref, ygb_ref, dskip_ref, snw_ref, mnw_ref, gnw_ref, mb_ref,
                  wbs_ref, wbm_ref, wbg_ref, wo_ref, lng_ref, lnb_ref, out_ref, *, alpha, ml_heads):
    d = h_ref.shape[1]
    ys = (ysf_ref[...] + ysb_ref[...] + dskip_ref[...] * xs_ref[...]) * _silu(z_ref[...])
    ys = _group_norm(ys, SSD_GROUPS, False) * snw_ref[...]
    b_ssd = _dot(ys.astype(BF16), wbs_ref[...])
    ym = _sigmoid(o_ref[...]) * (ymf_ref[...] + ymb_ref[...])
    ym = _group_norm(ym, ml_heads, True) * mnw_ref[...]
    b_ml = _dot(ym.astype(BF16), wbm_ref[...])
    yg = _group_norm(ygf_ref[...] + ygb_ref[...], GLA_HEADS, True) * gnw_ref[...] * _silu(r_ref[...])
    b_gla = _dot(yg.astype(BF16), wbg_ref[...])
    mb = mb_ref[...]
    mix = (_sigmoid(gate_ref[:, 0:d] + mb[0:1]) * b_ssd
           + _sigmoid(gate_ref[:, d:2 * d] + mb[1:2]) * b_ml
           + _sigmoid(gate_ref[:, 2 * d:3 * d] + mb[2:3]) * b_gla)
    y = _dot(mix.astype(BF16), wo_ref[...])
    m = mod_ref[0]
    out_ref[...] = _layer_norm(alpha * h_ref[...] + m[2:3] * y, lng_ref[...], lnb_ref[...])


def kernel(x, c, ctx, c_ctx, w_mod, b_mod, ln_g, ln_b, ffn_w_in, ffn_w_out, w_in, merge_b, ssd_conv_w, ssd_conv_b, ssd_dt_bias, ssd_a_log, ssd_d, ssd_norm_w, ml_conv_w, ml_conv_b, ml_gate_b, ml_norm_w, gla_w2, gla_b2, gla_norm_w, w_br_ssd, w_br_ml, w_br_gla, w_out):
    batch, seq, d = x.shape
    ctx_len = ctx.shape[1]
    depth = w_mod.shape[0]
    alpha = (2 * depth) ** 0.25
    ssd_inner = ssd_norm_w.shape[1]
    ssd_heads = ssd_dt_bias.shape[2]
    ssd_hdim = ssd_inner // ssd_heads
    bc_w = 2 * SSD_GROUPS * SSD_STATE
    ml_w = ml_norm_w.shape[1]
    ml_heads = ml_gate_b.shape[3]
    gla_w = gla_norm_w.shape[1]
    gla_kw = gla_w2.shape[3]
    gla_rank = gla_w2.shape[2]
    n_lat = batch * seq
    n_all = n_lat + batch * ctx_len
    tb = SCAN_BLOCK
    assert seq % tb == 0 and ctx_len == tb
    assert 2 * ssd_heads + 4 * ml_heads + 2 * gla_rank <= SMALL_W

    tm = next(t for t in (1024, 512, tb) if seq % t == 0 and (batch * ctx_len) % t == 0)
    tm_merge = 256
    tf = 512

    def group_of_tile_for(t):
        per_batch = seq // t
        return lambda i: jnp.minimum(i // per_batch, batch)

    assert (d, ssd_inner, ml_w, gla_w, gla_kw, bc_w) == (2048, 1024, 512, 512, 256, 512)
    gate_w = 3 * d
    col_z = gate_w
    col_mlv = col_z + ssd_inner
    col_mlqk = col_mlv + ml_w
    col_xs = col_mlqk + 2 * ml_w
    col_bc = col_xs + ssd_inner
    col_mlo = col_bc + bc_w
    col_glav = col_mlo + ml_w
    col_glar = col_glav + gla_w
    col_glaq = col_glar + gla_w
    col_glak = col_glaq + gla_kw
    col_small = col_glak + gla_kw
    tn_proj = 1280
    n_proj = -(-(col_small + 2 * SMALL_W) // tn_proj) * tn_proj
    conv_w = 2 * ml_w + ssd_inner + bc_w
    tc_conv = conv_w // 2
    assert col_mlqk % tc_conv == 0

    in_sizes = (ssd_inner, ssd_inner + bc_w, 2 * ssd_heads, ml_w, ml_w, ml_w, ml_w, 4 * ml_heads,
                gla_kw, gla_kw, gla_w, gla_w, 2 * gla_rank, gate_w)
    offs = [0]
    for sz in in_sizes:
        offs.append(offs[-1] + sz)
    (o_z, o_xbc, o_dt, o_mlq, o_mlk, o_mlv, o_mlo, o_mlg, o_gq, o_gk, o_gv, o_gr, o_lr, o_gate, o_end) = offs

    rows_mod = 8
    cpad = jnp.concatenate([c, c_ctx[None], jnp.zeros((rows_mod - batch - 1, d), F32)], axis=0)
    mod = _modulation(cpad, w_mod, b_mod).reshape(depth, rows_mod, N_MOD, d)

    ffn_in_bf = ffn_w_in.astype(BF16)
    ffn_out_bf = ffn_w_out.astype(BF16)
    w_br_bf = [w.astype(BF16) for w in (w_br_ssd, w_br_ml, w_br_gla)]
    w_out_bf = w_out.astype(BF16)
    g0 = 2 * ssd_heads
    small_used = 2 * ssd_heads + 4 * ml_heads + 2 * gla_rank
    fgate = []
    for dd in range(2):
        f0 = o_mlg + (2 * dd + 1) * ml_heads
        fgate += [(f0, f0 + ml_heads), ml_heads]
    w_big = _permute_cast(
        w_in,
        [(o_gate, o_end), (o_z, o_xbc), (o_mlv, o_mlo), (o_mlq, o_mlv), (o_xbc, o_dt), (o_mlo, o_mlg), (o_gv, o_lr),
         (o_gq, o_gv), (o_dt, o_mlq), (o_mlg, o_gq), (o_lr, o_gate), SMALL_W - small_used, g0] + fgate
        + [n_proj - col_small - SMALL_W - g0 - 4 * ml_heads])

    h = None
    for l in range(depth):
        last = l == depth - 1
        n_out = n_lat if last else n_all

        conv_w9 = jnp.concatenate([ml_conv_w[l].reshape(9, 2 * ml_w), ssd_conv_w[l].reshape(9, ssd_inner + bc_w)], axis=1)
        conv_b = jnp.concatenate([ml_conv_b[l], ssd_conv_b[l]])[None]

        def small_row(vals, start):
            return jnp.zeros((1, SMALL_W), F32).at[0, start:start + vals.shape[0]].set(vals)

        dt_bias_row = small_row(ssd_dt_bias[l].reshape(-1), 0)
        a_neg_row = small_row(-jnp.exp(ssd_a_log[l].astype(F32)).reshape(-1), 0)
        gate_bi_row = small_row(ml_gate_b[l].reshape(-1), g0)
        gate_bf_row = small_row(jnp.concatenate([ml_gate_b[l, :, 1], jnp.zeros((2, ml_heads), F32)], axis=1).reshape(-1), g0)
        lr0 = 2 * ssd_heads + 4 * ml_heads
        w2_pad = [jnp.zeros((SMALL_W, gla_kw), F32).at[lr0 + dd * gla_rank:lr0 + (dd + 1) * gla_rank].set(gla_w2[l, dd])
                  for dd in range(2)]

        ffn0 = functools.partial(_ffn, mod3=mod[l, :, 0:3], group_of_tile=group_of_tile_for(tm), w_in_bf=ffn_in_bf,
                                 w_out_bf=ffn_out_bf, lk=(l, 0), ln_g=ln_g[l, 0], ln_b=ln_b[l, 0], alpha=alpha,
                                 tm=tm, tf=tf)
        if h is None:
            h = ffn0(x.reshape(n_lat, d), n_lat, out_rows=n_all)
            h = ffn0(ctx.reshape(batch * ctx_len, d), batch * ctx_len, out_rows=n_all, tile0=n_lat // tm, out_buf=h)
        else:
            h = ffn0(h, n_all)

        p = _proj(h, mod[l, :, 3:6], group_of_tile_for(tm), w_big, l, tm, tn_proj)
        cv = _conv(p, col_mlqk // tc_conv, conv_w9, conv_b, n_lat, seq, tc_conv)
        col_cvb = 2 * ml_w + ssd_inner
        ys, ym, yg = [], [], []
        for dd, reverse in ((0, False), (1, True)):
            rows, steps = _scan_rows(batch, seq, ctx_len, reverse)
            ys.append(_scan_call(
                functools.partial(_ssd_kernel, batch=batch, d=dd, reverse=reverse, heads=ssd_heads, hdim=ssd_hdim),
                "ssd_scan", batch, rows, steps,
                [(cv, ssd_inner, 2 * ml_w // ssd_inner), (cv, bc_w // 2, col_cvb // (bc_w // 2)),
                 (cv, bc_w // 2, col_cvb // (bc_w // 2) + 1), (p, SMALL_W, col_small // SMALL_W)],
                [dt_bias_row, a_neg_row], ssd_inner,
                [pltpu.VMEM((batch, SSD_GROUPS, SSD_STATE, ssd_inner // SSD_GROUPS), F32)]))
            ym.append(_scan_call(
                functools.partial(_ml_kernel, batch=batch, d=dd, reverse=reverse, heads=ml_heads,
                                  gate_col0=2 * ssd_heads),
                "mlstm_scan", batch, rows, steps,
                [(cv, ml_w, 0), (cv, ml_w, 1), (p, ml_w, col_mlv // ml_w), (p, SMALL_W, col_small // SMALL_W),
                 (p, SMALL_W, col_small // SMALL_W + 1)],
                [gate_bi_row, gate_bf_row], ml_w,
                [pltpu.VMEM((batch, ml_heads, ml_w // ml_heads, 2 * ml_w // ml_heads), F32),
                 pltpu.VMEM((batch, ml_heads, 1, ml_w // ml_heads), F32)]))
            yg.append(_scan_call(
                functools.partial(_gla_kernel, batch=batch, reverse=reverse, heads=GLA_HEADS),
                "gla_scan", batch, rows, steps,
                [(p, gla_kw, col_glaq // gla_kw), (p, gla_kw, col_glak // gla_kw), (p, gla_w, col_glav // gla_w),
                 (p, SMALL_W, col_small // SMALL_W)],
                [w2_pad[dd].astype(BF16), gla_b2[l, dd][None]], gla_w,
                [pltpu.VMEM((batch, GLA_HEADS, gla_w // GLA_HEADS, LANE), F32)]))

        tmm = tm_merge
        row = lambda i: (i, 0)
        const = lambda i: (0, 0)
        y_fwd = _scan_out_rows(batch, seq, ctx_len, False, tmm)
        y_bwd = _scan_out_rows(batch, seq, ctx_len, True, tmm)
        row_f = lambda i: (y_fwd(i), 0)
        row_b = lambda i: (y_bwd(i), 0)

        def col(cb):
            return lambda i: (i, cb)

        g_of = group_of_tile_for(tmm)
        consts = [jnp.repeat(ssd_d[l], ssd_hdim)[None], ssd_norm_w[l][None], ml_norm_w[l][None], gla_norm_w[l][None],
                  merge_b[l]]
        weights = [w_br_bf[0], w_br_bf[1], w_br_bf[2], w_out_bf]
        ln_consts = [ln_g[l, 1][None], ln_b[l, 1][None]]
        h_mix = pl.pallas_call(
            functools.partial(_merge_kernel, alpha=alpha, ml_heads=ml_heads),
            grid=(n_out // tmm,),
            in_specs=[pl.BlockSpec((tmm, d), row),
                      pl.BlockSpec((1, 3, d), lambda i: (g_of(i), 0, 0)),
                      pl.BlockSpec((tmm, gate_w), col(0)),
                      pl.BlockSpec((tmm, ssd_inner), col(col_z // ssd_inner)),
                      pl.BlockSpec((tmm, ssd_inner), col(2 * ml_w // ssd_inner)),
                      pl.BlockSpec((tmm, ssd_inner), row_f),
                      pl.BlockSpec((tmm, ssd_inner), row_b),
                      pl.BlockSpec((tmm, ml_w), col(col_mlo // ml_w)),
                      pl.BlockSpec((tmm, ml_w), row_f),
                      pl.BlockSpec((tmm, ml_w), row_b),
                      pl.BlockSpec((tmm, gla_w), col(col_glar // gla_w)),
                      pl.BlockSpec((tmm, gla_w), row_f),
                      pl.BlockSpec((tmm, gla_w), row_b)]
                     + [pl.BlockSpec(a.shape, const) for a in consts]
                     + [pl.BlockSpec((None,) + a.shape[1:], functools.partial(lambda i, l: (l, 0, 0), l=l),
                                     pipeline_mode=pl.Buffered(1))
                        for a in weights]
                     + [pl.BlockSpec(a.shape, const) for a in ln_consts],
            out_specs=pl.BlockSpec((tmm, d), row),
            out_shape=jax.ShapeDtypeStruct((n_out, d), F32),
            compiler_params=_cparams(("arbitrary",)),
            name="merge_out",
        )(h, mod[l, :, 3:6], p, p, cv, ys[0], ys[1], p, ym[0], ym[1], p, yg[0], yg[1], *consts, *weights, *ln_consts)

        h = _ffn(h_mix, n_out, mod[l, :, 6:9], group_of_tile_for(tm), ffn_in_bf, ffn_out_bf, (l, 1), ln_g[l, 2],
                 ln_b[l, 2], alpha, tm, tf)

    return h.reshape(batch, seq, d)
```

```python
import functools

import jax
import jax.numpy as jnp
from jax import lax
from jax.experimental import pallas as pl
from jax.experimental.pallas import tpu as pltpu

F32 = jnp.float32
BF16 = jnp.bfloat16
HIGHEST = lax.Precision.HIGHEST

GRID_W = 64
CHUNK = 64
SSD_GROUPS = 4
SSD_STATE = 64
GLA_HEADS = 4
GLA_TAU = 16.0
EPS = 1e-5
N_MOD = 9

SCAN_BLOCK = 256
SMALL_W = 128
V7X_VMEM_LIMIT = 56 * 1024 * 1024
V7X_VMEM_LIMIT_BIG = 60 * 1024 * 1024

NT_DIMS = (((1,), (1,)), ((), ()))
TN_DIMS = (((0,), (0,)), ((), ()))


def _cparams(sem, vmem=V7X_VMEM_LIMIT):
    return pltpu.CompilerParams(dimension_semantics=sem, vmem_limit_bytes=vmem)


def _dot(a, b):
    return jnp.dot(a, b, preferred_element_type=F32)


def _dot_exact(a, b):
    return jnp.dot(a, b, preferred_element_type=F32, precision=HIGHEST)


def _dot_nt(a, b):
    return lax.dot_general(a, b, NT_DIMS, preferred_element_type=F32)


def _dot_tn(a, b):
    return lax.dot_general(a, b, TN_DIMS, preferred_element_type=F32)


def _sigmoid(x):
    return 1.0 / (1.0 + jnp.exp(-x))


def _silu(x):
    return x * _sigmoid(x)


def _softplus(x):
    return jnp.maximum(x, 0.0) + jnp.log(1.0 + jnp.exp(-jnp.abs(x)))


def _log_sigmoid(x):
    return -_softplus(-x)


def _layer_norm(x, g, b):
    mu = jnp.mean(x, axis=-1, keepdims=True)
    xc = x - mu
    var = jnp.mean(xc * xc, axis=-1, keepdims=True)
    return xc * lax.rsqrt(var + EPS) * g + b


def _mod_kernel(c_ref, w_ref, b_ref, o_ref):
    c = c_ref[...]
    o_ref[0] = _dot(_silu(c).astype(BF16), w_ref[0].astype(BF16)) + b_ref[0]


def _modulation(cpad, w_mod, b_mod, tn=1152):
    depth, d, n = w_mod.shape
    rows = cpad.shape[0]
    return pl.pallas_call(
        _mod_kernel,
        grid=(depth, n // tn),
        in_specs=[pl.BlockSpec((rows, d), lambda l, j: (0, 0)),
                  pl.BlockSpec((1, d, tn), lambda l, j: (l, 0, j)),
                  pl.BlockSpec((1, 1, tn), lambda l, j: (l, 0, j))],
        out_specs=pl.BlockSpec((1, rows, tn), lambda l, j: (l, 0, j)),
        out_shape=jax.ShapeDtypeStruct((depth, rows, n), F32),
        compiler_params=_cparams(("arbitrary", "arbitrary")),
        name="modulation",
    )(cpad, w_mod, b_mod.reshape(depth, 1, n))


def _ffn_kernel(h_ref, mod_ref, wa_ref, wg_ref, wo_ref, lng_ref, lnb_ref, o_ref, *, alpha):
    j = pl.program_id(1)
    m = mod_ref[0]
    u = (h_ref[...] * (1.0 + m[1:2]) + m[0:1]).astype(BF16)
    a = _dot(u, wa_ref[...])
    g = _dot(u, wg_ref[...])

    @pl.when(j == 0)
    def _():
        o_ref[...] = jnp.zeros_like(o_ref)

    o_ref[...] += _dot((_silu(g) * a).astype(BF16), wo_ref[...])

    @pl.when(j == pl.num_programs(1) - 1)
    def _():
        x = alpha * h_ref[...] + m[2:3] * (0.5 * o_ref[...])
        o_ref[...] = _layer_norm(x, lng_ref[...], lnb_ref[...])


def _ffn_kernel_inplace(h_ref, mod_ref, wa_ref, wg_ref, wo_ref, lng_ref, lnb_ref, buf_ref, o_ref, *, alpha):
    del buf_ref
    _ffn_kernel(h_ref, mod_ref, wa_ref, wg_ref, wo_ref, lng_ref, lnb_ref, o_ref, alpha=alpha)


def _ffn(h, n_rows, mod3, group_of_tile, w_in_bf, w_out_bf, lk, ln_g, ln_b, alpha, tm, tf, out_rows=None, tile0=0,
         out_buf=None):
    d = h.shape[1]
    dff = w_out_bf.shape[2]
    nf = dff // tf
    l, k = lk
    out_rows = n_rows if out_rows is None else out_rows
    in_specs = [pl.BlockSpec((tm, d), lambda i, j: (i, 0)),
                pl.BlockSpec((1, 3, d), lambda i, j: (group_of_tile(i + tile0), 0, 0)),
                pl.BlockSpec((None, None, d, tf), lambda i, j: (l, k, 0, j)),
                pl.BlockSpec((None, None, d, tf), lambda i, j: (l, k, 0, j + nf)),
                pl.BlockSpec((None, None, tf, d), lambda i, j: (l, k, j, 0)),
                pl.BlockSpec((1, d), lambda i, j: (0, 0)),
                pl.BlockSpec((1, d), lambda i, j: (0, 0))]
    args = [h, mod3, w_in_bf, w_in_bf, w_out_bf, ln_g.reshape(1, d), ln_b.reshape(1, d)]
    kern = functools.partial(_ffn_kernel, alpha=alpha)
    aliases = {}
    if out_buf is not None:
        in_specs.append(pl.BlockSpec(memory_space=pl.ANY))
        args.append(out_buf)
        aliases = {len(args) - 1: 0}
        kern = functools.partial(_ffn_kernel_inplace, alpha=alpha)
    return pl.pallas_call(
        kern,
        grid=(n_rows // tm, nf),
        in_specs=in_specs,
        out_specs=pl.BlockSpec((tm, d), lambda i, j: (i + tile0, 0)),
        out_shape=jax.ShapeDtypeStruct((out_rows, d), F32),
        input_output_aliases=aliases,
        compiler_params=_cparams(("arbitrary", "arbitrary"), V7X_VMEM_LIMIT_BIG),
        name="ffn",
    )(*args)


def _proj_kernel(h_ref, mod_ref, w_ref, o_ref, u_scr):
    @pl.when(pl.program_id(1) == 0)
    def _():
        m = mod_ref[0]
        u_scr[...] = (h_ref[...] * (1.0 + m[1:2]) + m[0:1]).astype(BF16)

    o_ref[...] = _dot_nt(u_scr[...], w_ref[...])


def _proj(h, mod3, group_of_tile, w_big, l, tm, tn):
    n_rows, d = h.shape
    n = w_big.shape[1]
    return pl.pallas_call(
        _proj_kernel,
        grid=(n_rows // tm, n // tn),
        in_specs=[pl.BlockSpec((tm, d), lambda i, j: (i, 0)),
                  pl.BlockSpec((1, 3, d), lambda i, j: (group_of_tile(i), 0, 0)),
                  pl.BlockSpec((None, tn, d), lambda i, j: (l, j, 0))],
        out_specs=pl.BlockSpec((tm, tn), lambda i, j: (i, j)),
        out_shape=jax.ShapeDtypeStruct((n_rows, n), F32),
        scratch_shapes=[pltpu.VMEM((tm, d), BF16)],
        compiler_params=_cparams(("arbitrary", "arbitrary")),
        name="in_proj",
    )(h, mod3, w_big)


def _conv_kernel(c_ref, p_ref, n_ref, w_ref, b_ref, o_ref, *, n_lat_blocks, blocks_per_seq):
    i = pl.program_id(0)
    tb = c_ref.shape[0]
    is_lat = i < n_lat_blocks
    in_seq = i % blocks_per_seq
    keep_prev = jnp.logical_and(is_lat, in_seq != 0)
    keep_next = jnp.logical_and(is_lat, in_seq != blocks_per_seq - 1)
    cur = c_ref[...]
    up = jnp.concatenate([jnp.where(keep_prev, p_ref[...], 0.0), cur[:tb - GRID_W]], axis=0)
    down = jnp.concatenate([cur[GRID_W:], jnp.where(keep_next, n_ref[...], 0.0)], axis=0)
    w = w_ref[...]
    vert = jnp.where(is_lat, 1.0, 0.0)
    w_up, w_mid, w_dn = w[0:3] * vert, w[3:6], w[6:9] * vert

    def column(dw):
        return w_up[dw:dw + 1] * up + w_mid[dw:dw + 1] * cur + w_dn[dw:dw + 1] * down

    r = lax.broadcasted_iota(jnp.int32, (tb, 1), 0)
    row_len = jnp.where(is_lat, GRID_W, tb)
    pos = r & (row_len - 1)
    acc = (column(1) + b_ref[...]
           + jnp.where(pos != 0, pltpu.roll(column(0), 1, 0), 0.0)
           + jnp.where(pos != row_len - 1, pltpu.roll(column(2), tb - 1, 0), 0.0))
    o_ref[...] = _silu(acc)


def _conv(p, col0_blocks, w9, bias, n_lat_rows, seq, tc):
    n_rows = p.shape[0]
    ch = w9.shape[1]
    tb = SCAN_BLOCK
    halo_per_blk = tb // GRID_W
    n_halo_rows = n_rows // GRID_W
    kern = functools.partial(_conv_kernel, n_lat_blocks=n_lat_rows // tb, blocks_per_seq=seq // tb)
    return pl.pallas_call(
        kern,
        grid=(n_rows // tb, ch // tc),
        in_specs=[pl.BlockSpec((tb, tc), lambda i, j: (i, col0_blocks + j)),
                  pl.BlockSpec((GRID_W, tc), lambda i, j: (jnp.maximum(i * halo_per_blk - 1, 0), col0_blocks + j)),
                  pl.BlockSpec((GRID_W, tc),
                               lambda i, j: (jnp.minimum((i + 1) * halo_per_blk, n_halo_rows - 1), col0_blocks + j)),
                  pl.BlockSpec((9, tc), lambda i, j: (0, j)),
                  pl.BlockSpec((1, tc), lambda i, j: (0, j))],
        out_specs=pl.BlockSpec((tb, tc), lambda i, j: (i, j)),
        out_shape=jax.ShapeDtypeStruct((n_rows, ch), F32),
        compiler_params=_cparams(("arbitrary", "arbitrary")),
        name="short_conv",
    )(p, p, p, w9, bias)


def _cumsum_op(tb, length, reverse):
    r = lax.broadcasted_iota(jnp.int32, (tb, tb), 0)
    c = lax.broadcasted_iota(jnp.int32, (tb, tb), 1)
    shift = length.bit_length() - 1
    same = (r >> shift) == (c >> shift)
    return jnp.logical_and(same, (c >= r) if reverse else (c <= r)).astype(BF16)


def _apply_op(op, x):
    hi = x.astype(BF16)
    r1 = x - hi.astype(F32)
    mid = r1.astype(BF16)
    lo = (r1 - mid.astype(F32)).astype(BF16)
    n = x.shape[1]
    out = _dot(op, jnp.concatenate([hi, mid, lo], axis=1))
    return out[:, 0:n] + out[:, n:2 * n] + out[:, 2 * n:3 * n]


def _causal(length, reverse):
    r = lax.broadcasted_iota(jnp.int32, (length, length), 0)
    c = lax.broadcasted_iota(jnp.int32, (length, length), 1)
    return (c >= r) if reverse else (c <= r)


def _chunk_order(tb, reverse):
    idx = list(range(tb // CHUNK))
    return idx[::-1] if reverse else idx


LANE = 128
LOG2_E = 1.4426950408889634


def _half_masks():
    lane_lo = lax.broadcasted_iota(jnp.int32, (1, LANE), 1) < LANE // 2
    sub_lo = lax.broadcasted_iota(jnp.int32, (LANE, 1), 0) < LANE // 2
    return lane_lo, sub_lo


def _ssd_block(xs_ref, bm_ref, cm_ref, sm_ref, bias, aneg, y_ref, row0, s_ref, op, lmask, *, d, reverse, heads, hdim):
    tb = xs_ref.shape[0]
    hpg = heads // SSD_GROUPS
    assert hdim == LANE // 2 and SSD_STATE == LANE // 2 and hpg % 2 == 0 and SSD_GROUPS % 2 == 0
    end = 0 if reverse else tb - 1
    lane_lo, sub_lo = _half_masks()
    dt = _softplus(sm_ref[...] + bias)
    acum2 = _apply_op(op, dt * aneg) * LOG2_E
    acum2_t = acum2.T
    src_t = (acum2 - jnp.log2(dt)).T
    w_t = jnp.exp2(acum2_t[:, end:end + 1] - src_t)
    e_end = jnp.exp2(acum2[end:end + 1, :])
    xs = xs_ref[...].astype(BF16)
    bm_t = bm_ref[...].T
    cm = cm_ref[...].astype(BF16)
    for gp in range(SSD_GROUPS // 2):
        cm_pair = cm[:, gp * LANE:(gp + 1) * LANE]
        bm_pair_t = bm_t[gp * LANE:(gp + 1) * LANE, :]
        for gi in range(2):
            g = 2 * gp + gi
            in_g = sub_lo if gi == 0 else jnp.logical_not(sub_lo)
            cb = _dot(cm_pair, jnp.where(in_g, bm_pair_t, 0.0).astype(BF16))
            bmg_t = bm_t[g * SSD_STATE:(g + 1) * SSD_STATE, :]
            s_g = s_ref[g]
            zero = jnp.zeros_like(s_g)
            s_rows = jnp.concatenate([s_g, zero] if gi == 0 else [zero, s_g], axis=0).astype(BF16)
            y_in = _dot(cm_pair, s_rows)
            for pi in range(hpg // 2):
                gl = slice(pi * LANE, (pi + 1) * LANE)
                xl = slice(g * hpg * hdim + pi * LANE, g * hpg * hdim + (pi + 1) * LANE)
                xs_pair = xs[:, xl]
                xs_half = (jnp.where(lane_lo, xs_pair, 0.0).astype(BF16), jnp.where(lane_lo, 0.0, xs_pair).astype(BF16))
                c0 = d * heads + g * hpg + 2 * pi
                y_acc = None
                s_acc = None
                e_col = []
                for hi in range(2):
                    c = c0 + hi
                    colb = jnp.broadcast_to(acum2[:, c:c + 1], (tb, tb))
                    m = cb * jnp.exp2(jnp.where(lmask, colb - src_t[c:c + 1, :], -jnp.inf))
                    y_h = _dot(m.astype(BF16), xs_half[hi])
                    s_h = _dot((bmg_t * w_t[c:c + 1, :]).astype(BF16), xs_half[hi])
                    y_acc = y_h if y_acc is None else y_acc + y_h
                    s_acc = s_h if s_acc is None else s_acc + s_h
                    e_col.append(jnp.exp2(colb[:, 0:LANE]))
                y_ref[row0:row0 + tb, xl] = y_acc + y_in[:, gl] * jnp.where(lane_lo, e_col[0], e_col[1])
                e_pair = jnp.where(lane_lo, e_end[:, c0:c0 + 1], e_end[:, c0 + 1:c0 + 2])
                s_ref[g, :, gl] = s_g[:, gl] * e_pair + s_acc
                yield


def _ssd_kernel(*refs, batch, d, reverse, heads, hdim):
    ins, (bias_ref, aneg_ref, y_ref, s_scr) = refs[:4 * batch], refs[4 * batch:]

    @pl.when(pl.program_id(0) == 0)
    def _():
        s_scr[...] = jnp.zeros_like(s_scr)

    tb = SCAN_BLOCK
    op = _cumsum_op(tb, tb, reverse)
    lmask = _causal(tb, reverse)
    _interleave(_ssd_block(*ins[4 * b:4 * b + 4], bias_ref[...], aneg_ref[...], y_ref, b * tb, s_scr.at[b], op, lmask,
                           d=d, reverse=reverse, heads=heads, hdim=hdim) for b in range(batch))


def _ml_kernel(*refs, batch, d, reverse, heads, gate_col0):
    ins, (gbi_ref, gbf_ref, y_ref, ct_scr, m_scr) = refs[:5 * batch], refs[5 * batch:]

    @pl.when(pl.program_id(0) == 0)
    def _():
        ct_scr[...] = jnp.zeros_like(ct_scr)
        m_scr[...] = jnp.zeros_like(m_scr)

    tb = SCAN_BLOCK
    op = _cumsum_op(tb, tb, reverse)
    lmask = _causal(tb, reverse)
    states = [([ct_scr[b, h] for h in range(heads)], [m_scr[b, h][:, 0:1] for h in range(heads)]) for b in range(batch)]
    new = [([], []) for _ in range(batch)]
    _interleave(_ml_block(*ins[5 * b:5 * b + 5], gbi_ref[...], gbf_ref[...], y_ref, b * tb, states[b][0], states[b][1],
                          new[b][0], new[b][1], op, lmask, d=d, reverse=reverse, heads=heads, gate_col0=gate_col0)
                for b in range(batch))
    for b in range(batch):
        for h in range(heads):
            ct_scr[b, h] = new[b][0][h]
            m_scr[b, h] = jnp.broadcast_to(new[b][1][h], m_scr.shape[2:])


def _cummax_rows(x, reverse):
    n = x.shape[0]
    row = lax.broadcasted_iota(jnp.int32, (n, 1), 0)
    s = 1
    while s < n:
        if s < 8:
            if reverse:
                shifted = jnp.where(row < n - s, pltpu.roll(x, n - s, 0), -jnp.inf)
            else:
                shifted = jnp.where(row >= s, pltpu.roll(x, s, 0), -jnp.inf)
        else:
            pad = jnp.full((s, x.shape[1]), -jnp.inf, F32)
            shifted = jnp.concatenate([x[s:], pad] if reverse else [pad, x[:n - s]], axis=0)
        x = jnp.maximum(x, shifted)
        s *= 2
    return x


def _ml_block(q_ref, k_ref, v_ref, smi_ref, smf_ref, gate_bi, gate_bf, y_ref, row0, ct_in, m_in, ct_out, m_out, op, lmask,
              *, d, reverse, heads, gate_col0):
    tb = q_ref.shape[0]
    hd = q_ref.shape[1] // heads
    assert hd == LANE
    end = 0 if reverse else tb - 1
    base = gate_col0 + d * 2 * heads
    raw = smi_ref[...] + gate_bi
    bcum = _apply_op(op, _log_sigmoid(smf_ref[...] + gate_bf))
    r = raw - bcum
    r_t = r.T
    cmax = _cummax_rows(r, reverse)
    q_all = (q_ref[...] * (hd ** -0.5)).astype(BF16)
    k_all = k_ref[...].astype(BF16)
    v_all = v_ref[...].astype(BF16)
    ones = jnp.ones((tb, hd), BF16)
    yield
    for h in range(heads):
        c = base + h
        hs = slice(h * hd, (h + 1) * hd)
        qb = q_all[:, hs]
        k_t = k_all[:, hs].T
        v_ext = jnp.concatenate([v_all[:, hs], ones], axis=1)
        m_st = m_in[h]
        ct = ct_in[h]
        r_row = r_t[c:c + 1, :]
        b_rep = jnp.broadcast_to(bcum[:, c:c + 1], (tb, hd))
        u_rep = jnp.broadcast_to(jnp.maximum(cmax[:, c:c + 1], m_st), (tb, hd))
        s = jnp.exp(jnp.where(lmask, r_row - jnp.concatenate([u_rep, u_rep], axis=1), -jnp.inf)) * _dot(qb, k_t)
        g_rep = jnp.exp(m_st - u_rep)
        acc = _dot(s.astype(BF16), v_ext) + jnp.concatenate([g_rep, g_rep], axis=1) * _dot(qb, ct.astype(BF16))
        y_ref[row0:row0 + tb, hs] = acc[:, 0:hd] / jnp.maximum(jnp.abs(acc[:, hd:]), jnp.exp(-(b_rep + u_rep)))
        b_end = bcum[end:end + 1, c:c + 1]
        a_row = b_end + r_row
        m_loc = jnp.max(a_row, axis=1, keepdims=True)
        upd = _dot((k_t * jnp.exp(a_row - m_loc)).astype(BF16), v_ext)
        m_new = jnp.maximum(b_end + m_st, m_loc)
        ct_out.append(jnp.exp(b_end + m_st - m_new) * ct + jnp.exp(m_loc - m_new) * upd)
        m_out.append(m_new)
        yield


def _gla_block(q_ref, k_ref, v_ref, sm_ref, w2, b2, y_ref, row0, s_scr, op, lmask, *, reverse, heads):
    tb = q_ref.shape[0]
    dk = q_ref.shape[1] // heads
    dv = v_ref.shape[1] // heads
    assert dk == LANE // 2 and heads % 2 == 0
    lane_lo, _ = _half_masks()
    x = _dot(sm_ref[...].astype(BF16), w2) + b2
    b = _apply_op(op, _log_sigmoid(x) * (1.0 / GLA_TAU))
    q_d = (q_ref[...] * (dk ** -0.5) * jnp.exp(b)).astype(BF16)
    k = k_ref[...]
    k_d = k * jnp.exp(-b)
    v = v_ref[...].astype(BF16)
    state = [s_scr[h] for h in range(heads)]
    for ci in _chunk_order(tb, reverse):
        lo = ci * CHUNK
        sl = slice(lo, lo + CHUNK)
        end = lo if reverse else lo + CHUNK - 1
        b_end = b[end:end + 1, :]
        kk = k[sl] * jnp.exp(b_end - b[sl])
        e_end = jnp.exp(b_end)
        att, inter = [], []
        for h in range(heads):
            pl_ = slice(h // 2 * LANE, (h // 2 + 1) * LANE)
            mine = lane_lo if h % 2 == 0 else jnp.logical_not(lane_lo)
            vs = slice(h * dv, (h + 1) * dv)
            q_pair = q_d[sl, pl_]
            k_m = jnp.where(mine, k_d[sl, pl_], 0.0).astype(BF16)
            kk_m = jnp.where(mine, kk[:, pl_], 0.0).astype(BF16)
            att.append(_dot_nt(q_pair, k_m))
            inter.append(_dot_nt(q_pair, state[h].astype(BF16)))
            state[h] = state[h] * e_end[:, pl_] + _dot_tn(v[sl, vs], kk_m)
        yield
        outs = [_dot(jnp.where(lmask, att[h], 0.0).astype(BF16), v[sl, h * dv:(h + 1) * dv]) + inter[h]
                for h in range(heads)]
        y_ref[row0 + lo:row0 + lo + CHUNK, :] = jnp.concatenate(outs, axis=1)
        yield
    for h in range(heads):
        s_scr[h] = state[h]


def _interleave(gens):
    gens = list(gens)
    while gens:
        alive = []
        for g in gens:
            try:
                next(g)
                alive.append(g)
            except StopIteration:
                pass
        gens = alive


def _gla_kernel(*refs, batch, reverse, heads):
    ins, (w2_ref, b2_ref, y_ref, s_scr) = refs[:4 * batch], refs[4 * batch:]

    @pl.when(pl.program_id(0) == 0)
    def _():
        s_scr[...] = jnp.zeros_like(s_scr)

    tb = SCAN_BLOCK
    op = _cumsum_op(tb, CHUNK, reverse)
    lmask = _causal(CHUNK, reverse)
    _interleave(_gla_block(*ins[4 * b:4 * b + 4], w2_ref[...], b2_ref[...], y_ref, b * tb, s_scr.at[b], op, lmask,
                           reverse=reverse, heads=heads) for b in range(batch))


def _scan_rows(batch, seq, ctx_len, reverse):
    tb = SCAN_BLOCK
    nc, nl = ctx_len // tb, seq // tb
    ctx_base = batch * seq // tb

    def rows(b, s):
        if reverse:
            return jnp.where(s < nc, ctx_base + b * nc + (nc - 1 - s), b * nl + (nl - 1 - (s - nc)))
        return jnp.where(s < nc, ctx_base + b * nc + s, b * nl + (s - nc))

    return rows, nc + nl


def _scan_out_rows(batch, seq, ctx_len, reverse, tile):
    tb = SCAN_BLOCK
    nc, nl = ctx_len // tb, seq // tb
    sub = tb // tile

    def rows(i):
        q, part = i // sub, i % sub
        is_lat = q < batch * nl
        qc = jnp.maximum(q - batch * nl, 0)
        b = jnp.where(is_lat, q // nl, qc // nc)
        k = jnp.where(is_lat, q % nl, qc % nc)
        n = jnp.where(is_lat, nl, nc)
        s = jnp.where(is_lat, nc, 0) + (n - 1 - k if reverse else k)
        return (s * batch + b) * sub + part

    return rows


def _scan_call(kern, name, batch, rows, steps, ins, consts, out_w, scratch):
    tb = SCAN_BLOCK
    in_specs = [pl.BlockSpec((tb, w), functools.partial(lambda s, b, cb: (rows(b, s), cb), b=b, cb=cb))
                for b in range(batch) for _, w, cb in ins]
    in_specs += [pl.BlockSpec(c.shape, lambda s: (0, 0)) for c in consts]
    return pl.pallas_call(
        kern,
        grid=(steps,),
        in_specs=in_specs,
        out_specs=pl.BlockSpec((batch * tb, out_w), lambda s: (s, 0)),
        out_shape=jax.ShapeDtypeStruct((steps * batch * tb, out_w), F32),
        scratch_shapes=scratch,
        compiler_params=_cparams(("arbitrary",)),
        name=name,
    )(*[a for _ in range(batch) for a, _, _ in ins], *consts)


def _group_norm(y, groups, center):
    gw = y.shape[1] // groups
    outs = []
    for g in range(groups):
        yg = y[:, g * gw:(g + 1) * gw]
        if center:
            yg = yg - jnp.mean(yg, axis=-1, keepdims=True)
        outs.append(yg * lax.rsqrt(jnp.mean(yg * yg, axis=-1, keepdims=True) + EPS))
    return jnp.concatenate(outs, axis=1)


def _merge_kernel(h_ref, mod_ref, gate_ref, z_ref, xs_ref, ysf_ref, ysb_ref, o_ref, ymf_ref, ymb_ref,
                  r_ref, ygf_ref, ygb_ref, dskip_ref, snw_ref, mnw_ref, gnw_ref, mb_ref,
                  wbs_ref, wbm_ref, wbg_ref, wo_ref, lng_ref, lnb_ref, out_ref, *, alpha, ml_heads):
    d = h_ref.shape[1]
    ys = (ysf_ref[...] + ysb_ref[...] + dskip_ref[...] * xs_ref[...]) * _silu(z_ref[...])
    ys = _group_norm(ys, SSD_GROUPS, False) * snw_ref[...]
    b_ssd = _dot(ys.astype(BF16), wbs_ref[...])
    ym = _sigmoid(o_ref[...]) * (ymf_ref[...] + ymb_ref[...])
    ym = _group_norm(ym, ml_heads, True) * mnw_ref[...]
    b_ml = _dot(ym.astype(BF16), wbm_ref[...])
    yg = _group_norm(ygf_ref[...] + ygb_ref[...], GLA_HEADS, True) * gnw_ref[...] * _silu(r_ref[...])
    b_gla = _dot(yg.astype(BF16), wbg_ref[...])
    mb = mb_ref[...]
    mix = (_sigmoid(gate_ref[:, 0:d] + mb[0:1]) * b_ssd
           + _sigmoid(gate_ref[:, d:2 * d] + mb[1:2]) * b_ml
           + _sigmoid(gate_ref[:, 2 * d:3 * d] + mb[2:3]) * b_gla)
    y = _dot(mix.astype(BF16), wo_ref[...])
    m = mod_ref[0]
    out_ref[...] = _layer_norm(alpha * h_ref[...] + m[2:3] * y, lng_ref[...], lnb_ref[...])


def kernel(x, c, ctx, c_ctx, w_mod, b_mod, ln_g, ln_b, ffn_w_in, ffn_w_out, w_in, merge_b, ssd_conv_w, ssd_conv_b, ssd_dt_bias, ssd_a_log, ssd_d, ssd_norm_w, ml_conv_w, ml_conv_b, ml_gate_b, ml_norm_w, gla_w2, gla_b2, gla_norm_w, w_br_ssd, w_br_ml, w_br_gla, w_out):
    batch, seq, d = x.shape
    ctx_len = ctx.shape[1]
    depth = w_mod.shape[0]
    alpha = (2 * depth) ** 0.25
    ssd_inner = ssd_norm_w.shape[1]
    ssd_heads = ssd_dt_bias.shape[2]
    ssd_hdim = ssd_inner // ssd_heads
    bc_w = 2 * SSD_GROUPS * SSD_STATE
    ml_w = ml_norm_w.shape[1]
    ml_heads = ml_gate_b.shape[3]
    gla_w = gla_norm_w.shape[1]
    gla_kw = gla_w2.shape[3]
    gla_rank = gla_w2.shape[2]
    n_lat = batch * seq
    n_all = n_lat + batch * ctx_len
    tb = SCAN_BLOCK
    assert seq % tb == 0 and ctx_len == tb
    assert 2 * ssd_heads + 4 * ml_heads + 2 * gla_rank <= SMALL_W

    tm = next(t for t in (1024, 512, tb) if seq % t == 0 and (batch * ctx_len) % t == 0)
    tm_merge = 256
    tf = 512

    def group_of_tile_for(t):
        per_batch = seq // t
        return lambda i: jnp.minimum(i // per_batch, batch)

    assert (d, ssd_inner, ml_w, gla_w, gla_kw, bc_w) == (2048, 1024, 512, 512, 256, 512)
    gate_w = 3 * d
    col_z = gate_w
    col_mlv = col_z + ssd_inner
    col_mlqk = col_mlv + ml_w
    col_xs = col_mlqk + 2 * ml_w
    col_bc = col_xs + ssd_inner
    col_mlo = col_bc + bc_w
    col_glav = col_mlo + ml_w
    col_glar = col_glav + gla_w
    col_glaq = col_glar + gla_w
    col_glak = col_glaq + gla_kw
    col_small = col_glak + gla_kw
    tn_proj = 1280
    n_proj = -(-(col_small + 2 * SMALL_W) // tn_proj) * tn_proj
    conv_w = 2 * ml_w + ssd_inner + bc_w
    tc_conv = conv_w // 2
    assert col_mlqk % tc_conv == 0

    in_sizes = (ssd_inner, ssd_inner + bc_w, 2 * ssd_heads, ml_w, ml_w, ml_w, ml_w, 4 * ml_heads,
                gla_kw, gla_kw, gla_w, gla_w, 2 * gla_rank, gate_w)
    offs = [0]
    for sz in in_sizes:
        offs.append(offs[-1] + sz)
    (o_z, o_xbc, o_dt, o_mlq, o_mlk, o_mlv, o_mlo, o_mlg, o_gq, o_gk, o_gv, o_gr, o_lr, o_gate, o_end) = offs

    rows_mod = 8
    cpad = jnp.concatenate([c, c_ctx[None], jnp.zeros((rows_mod - batch - 1, d), F32)], axis=0)
    mod = _modulation(cpad, w_mod, b_mod).reshape(depth, rows_mod, N_MOD, d)

    ffn_in_bf = ffn_w_in.astype(BF16)
    ffn_out_bf = ffn_w_out.astype(BF16)
    w_br_bf = [w.astype(BF16) for w in (w_br_ssd, w_br_ml, w_br_gla)]
    w_out_bf = w_out.astype(BF16)
    g0 = 2 * ssd_heads
    small_used = 2 * ssd_heads + 4 * ml_heads + 2 * gla_rank
    fgate = []
    for dd in range(2):
        f0 = o_mlg + (2 * dd + 1) * ml_heads
        fgate += [(f0, f0 + ml_heads), ml_heads]
    w_t = jnp.swapaxes(w_in, 1, 2)
    segments = ([(o_gate, o_end), (o_z, o_xbc), (o_mlv, o_mlo), (o_mlq, o_mlv), (o_xbc, o_dt), (o_mlo, o_mlg),
                 (o_gv, o_lr), (o_gq, o_gv), (o_dt, o_mlq), (o_mlg, o_gq), (o_lr, o_gate), SMALL_W - small_used, g0]
                + fgate + [n_proj - col_small - SMALL_W - g0 - 4 * ml_heads])
    w_big = jnp.concatenate(
        [jnp.zeros((depth, seg, d), BF16) if isinstance(seg, int) else w_t[:, seg[0]:seg[1]].astype(BF16)
         for seg in segments], axis=1)

    h = None
    for l in range(depth):
        last = l == depth - 1
        n_out = n_lat if last else n_all

        conv_w9 = jnp.concatenate([ml_conv_w[l].reshape(9, 2 * ml_w), ssd_conv_w[l].reshape(9, ssd_inner + bc_w)], axis=1)
        conv_b = jnp.concatenate([ml_conv_b[l], ssd_conv_b[l]])[None]

        def small_row(vals, start):
            return jnp.zeros((1, SMALL_W), F32).at[0, start:start + vals.shape[0]].set(vals)

        dt_bias_row = small_row(ssd_dt_bias[l].reshape(-1), 0)
        a_neg_row = small_row(-jnp.exp(ssd_a_log[l].astype(F32)).reshape(-1), 0)
        gate_bi_row = small_row(ml_gate_b[l].reshape(-1), g0)
        gate_bf_row = small_row(jnp.concatenate([ml_gate_b[l, :, 1], jnp.zeros((2, ml_heads), F32)], axis=1).reshape(-1), g0)
        lr0 = 2 * ssd_heads + 4 * ml_heads
        w2_pad = [jnp.zeros((SMALL_W, gla_kw), F32).at[lr0 + dd * gla_rank:lr0 + (dd + 1) * gla_rank].set(gla_w2[l, dd])
                  for dd in range(2)]

        ffn0 = functools.partial(_ffn, mod3=mod[l, :, 0:3], group_of_tile=group_of_tile_for(tm), w_in_bf=ffn_in_bf,
                                 w_out_bf=ffn_out_bf, lk=(l, 0), ln_g=ln_g[l, 0], ln_b=ln_b[l, 0], alpha=alpha,
                                 tm=tm, tf=tf)
        if h is None:
            h = ffn0(x.reshape(n_lat, d), n_lat, out_rows=n_all)
            h = ffn0(ctx.reshape(batch * ctx_len, d), batch * ctx_len, out_rows=n_all, tile0=n_lat // tm, out_buf=h)
        else:
            h = ffn0(h, n_all)

        p = _proj(h, mod[l, :, 3:6], group_of_tile_for(tm), w_big, l, tm, tn_proj)
        cv = _conv(p, col_mlqk // tc_conv, conv_w9, conv_b, n_lat, seq, tc_conv)
        col_cvb = 2 * ml_w + ssd_inner
        ys, ym, yg = [], [], []
        for dd, reverse in ((0, False), (1, True)):
            rows, steps = _scan_rows(batch, seq, ctx_len, reverse)
            ys.append(_scan_call(
                functools.partial(_ssd_kernel, batch=batch, d=dd, reverse=reverse, heads=ssd_heads, hdim=ssd_hdim),
                "ssd_scan", batch, rows, steps,
                [(cv, ssd_inner, 2 * ml_w // ssd_inner), (cv, bc_w // 2, col_cvb // (bc_w // 2)),
                 (cv, bc_w // 2, col_cvb // (bc_w // 2) + 1), (p, SMALL_W, col_small // SMALL_W)],
                [dt_bias_row, a_neg_row], ssd_inner,
                [pltpu.VMEM((batch, SSD_GROUPS, SSD_STATE, ssd_inner // SSD_GROUPS), F32)]))
            ym.append(_scan_call(
                functools.partial(_ml_kernel, batch=batch, d=dd, reverse=reverse, heads=ml_heads,
                                  gate_col0=2 * ssd_heads),
                "mlstm_scan", batch, rows, steps,
                [(cv, ml_w, 0), (cv, ml_w, 1), (p, ml_w, col_mlv // ml_w), (p, SMALL_W, col_small // SMALL_W),
                 (p, SMALL_W, col_small // SMALL_W + 1)],
                [gate_bi_row, gate_bf_row], ml_w,
                [pltpu.VMEM((batch, ml_heads, ml_w // ml_heads, 2 * ml_w // ml_heads), F32),
                 pltpu.VMEM((batch, ml_heads, 1, ml_w // ml_heads), F32)]))
            yg.append(_scan_call(
                functools.partial(_gla_kernel, batch=batch, reverse=reverse, heads=GLA_HEADS),
                "gla_scan", batch, rows, steps,
                [(p, gla_kw, col_glaq // gla_kw), (p, gla_kw, col_glak // gla_kw), (p, gla_w, col_glav // gla_w),
                 (p, SMALL_W, col_small // SMALL_W)],
                [w2_pad[dd].astype(BF16), gla_b2[l, dd][None]], gla_w,
                [pltpu.VMEM((batch, GLA_HEADS, gla_w // GLA_HEADS, LANE), F32)]))

        tmm = tm_merge
        row = lambda i: (i, 0)
        const = lambda i: (0, 0)
        y_fwd = _scan_out_rows(batch, seq, ctx_len, False, tmm)
        y_bwd = _scan_out_rows(batch, seq, ctx_len, True, tmm)
        row_f = lambda i: (y_fwd(i), 0)
        row_b = lambda i: (y_bwd(i), 0)

        def col(cb):
            return lambda i: (i, cb)

        g_of = group_of_tile_for(tmm)
        consts = [jnp.repeat(ssd_d[l], ssd_hdim)[None], ssd_norm_w[l][None], ml_norm_w[l][None], gla_norm_w[l][None],
                  merge_b[l]]
        weights = [w_br_bf[0], w_br_bf[1], w_br_bf[2], w_out_bf]
        ln_consts = [ln_g[l, 1][None], ln_b[l, 1][None]]
        h_mix = pl.pallas_call(
            functools.partial(_merge_kernel, alpha=alpha, ml_heads=ml_heads),
            grid=(n_out // tmm,),
            in_specs=[pl.BlockSpec((tmm, d), row),
                      pl.BlockSpec((1, 3, d), lambda i: (g_of(i), 0, 0)),
                      pl.BlockSpec((tmm, gate_w), col(0)),
                      pl.BlockSpec((tmm, ssd_inner), col(col_z // ssd_inner)),
                      pl.BlockSpec((tmm, ssd_inner), col(2 * ml_w // ssd_inner)),
                      pl.BlockSpec((tmm, ssd_inner), row_f),
                      pl.BlockSpec((tmm, ssd_inner), row_b),
                      pl.BlockSpec((tmm, ml_w), col(col_mlo // ml_w)),
                      pl.BlockSpec((tmm, ml_w), row_f),
                      pl.BlockSpec((tmm, ml_w), row_b),
                      pl.BlockSpec((tmm, gla_w), col(col_glar // gla_w)),
                      pl.BlockSpec((tmm, gla_w), row_f),
                      pl.BlockSpec((tmm, gla_w), row_b)]
                     + [pl.BlockSpec(a.shape, const) for a in consts]
                     + [pl.BlockSpec((None,) + a.shape[1:], functools.partial(lambda i, l: (l, 0, 0), l=l),
                                     pipeline_mode=pl.Buffered(1))
                        for a in weights]
                     + [pl.BlockSpec(a.shape, const) for a in ln_consts],
            out_specs=pl.BlockSpec((tmm, d), row),
            out_shape=jax.ShapeDtypeStruct((n_out, d), F32),
            compiler_params=_cparams(("arbitrary",)),
            name="merge_out",
        )(h, mod[l, :, 3:6], p, p, cv, ys[0], ys[1], p, ym[0], ym[1], p, yg[0], yg[1], *consts, *weights, *ln_consts)

        h = _ffn(h_mix, n_out, mod[l, :, 6:9], group_of_tile_for(tm), ffn_in_bf, ffn_out_bf, (l, 1), ln_g[l, 2],
                 ln_b[l, 2], alpha, tm, tf)

    return h.reshape(batch, seq, d)
```

```python
import functools

import jax
import jax.numpy as jnp
from jax import lax
from jax.experimental import pallas as pl
from jax.experimental.pallas import tpu as pltpu

F32 = jnp.float32
BF16 = jnp.bfloat16
HIGHEST = lax.Precision.HIGHEST

GRID_W = 64
CHUNK = 64
SSD_GROUPS = 4
SSD_STATE = 64
GLA_HEADS = 4
GLA_TAU = 16.0
EPS = 1e-5
N_MOD = 9

SCAN_BLOCK = 256
SMALL_W = 128
V7X_VMEM_LIMIT = 56 * 1024 * 1024
V7X_VMEM_LIMIT_BIG = 60 * 1024 * 1024

NT_DIMS = (((1,), (1,)), ((), ()))
TN_DIMS = (((0,), (0,)), ((), ()))


def _cparams(sem, vmem=V7X_VMEM_LIMIT):
    return pltpu.CompilerParams(dimension_semantics=sem, vmem_limit_bytes=vmem)


def _dot(a, b):
    return jnp.dot(a, b, preferred_element_type=F32)


def _dot_exact(a, b):
    return jnp.dot(a, b, preferred_element_type=F32, precision=HIGHEST)


def _dot_nt(a, b):
    return lax.dot_general(a, b, NT_DIMS, preferred_element_type=F32)


def _dot_tn(a, b):
    return lax.dot_general(a, b, TN_DIMS, preferred_element_type=F32)


def _sigmoid(x):
    return 1.0 / (1.0 + jnp.exp(-x))


def _silu(x):
    return x * _sigmoid(x)


def _softplus(x):
    return jnp.maximum(x, 0.0) + jnp.log(1.0 + jnp.exp(-jnp.abs(x)))


def _log_sigmoid(x):
    return -_softplus(-x)


def _layer_norm(x, g, b):
    mu = jnp.mean(x, axis=-1, keepdims=True)
    xc = x - mu
    var = jnp.mean(xc * xc, axis=-1, keepdims=True)
    return xc * lax.rsqrt(var + EPS) * g + b


def _mod_kernel(c_ref, w_ref, b_ref, o_ref):
    c = c_ref[...]
    o_ref[0] = _dot(_silu(c).astype(BF16), w_ref[0].astype(BF16)) + b_ref[0]


def _modulation(cpad, w_mod, b_mod, tn=1152):
    depth, d, n = w_mod.shape
    rows = cpad.shape[0]
    return pl.pallas_call(
        _mod_kernel,
        grid=(depth, n // tn),
        in_specs=[pl.BlockSpec((rows, d), lambda l, j: (0, 0)),
                  pl.BlockSpec((1, d, tn), lambda l, j: (l, 0, j)),
                  pl.BlockSpec((1, 1, tn), lambda l, j: (l, 0, j))],
        out_specs=pl.BlockSpec((1, rows, tn), lambda l, j: (l, 0, j)),
        out_shape=jax.ShapeDtypeStruct((depth, rows, n), F32),
        compiler_params=_cparams(("arbitrary", "arbitrary")),
        name="modulation",
    )(cpad, w_mod, b_mod.reshape(depth, 1, n))


def _ffn_kernel(h_ref, mod_ref, wa_ref, wg_ref, wo_ref, lng_ref, lnb_ref, o_ref, *, alpha):
    j = pl.program_id(1)
    m = mod_ref[0]
    u = (h_ref[...] * (1.0 + m[1:2]) + m[0:1]).astype(BF16)
    a = _dot(u, wa_ref[...])
    g = _dot(u, wg_ref[...])

    @pl.when(j == 0)
    def _():
        o_ref[...] = jnp.zeros_like(o_ref)

    o_ref[...] += _dot((_silu(g) * a).astype(BF16), wo_ref[...])

    @pl.when(j == pl.num_programs(1) - 1)
    def _():
        x = alpha * h_ref[...] + m[2:3] * (0.5 * o_ref[...])
        o_ref[...] = _layer_norm(x, lng_ref[...], lnb_ref[...])


def _ffn_kernel_inplace(h_ref, mod_ref, wa_ref, wg_ref, wo_ref, lng_ref, lnb_ref, buf_ref, o_ref, *, alpha):
    del buf_ref
    _ffn_kernel(h_ref, mod_ref, wa_ref, wg_ref, wo_ref, lng_ref, lnb_ref, o_ref, alpha=alpha)


def _ffn(h, n_rows, mod3, group_of_tile, w_in_bf, w_out_bf, lk, ln_g, ln_b, alpha, tm, tf, out_rows=None, tile0=0,
         out_buf=None):
    d = h.shape[1]
    dff = w_out_bf.shape[2]
    nf = dff // tf
    l, k = lk
    out_rows = n_rows if out_rows is None else out_rows
    in_specs = [pl.BlockSpec((tm, d), lambda i, j: (i, 0)),
                pl.BlockSpec((1, 3, d), lambda i, j: (group_of_tile(i + tile0), 0, 0)),
                pl.BlockSpec((None, None, d, tf), lambda i, j: (l, k, 0, j)),
                pl.BlockSpec((None, None, d, tf), lambda i, j: (l, k, 0, j + nf)),
                pl.BlockSpec((None, None, tf, d), lambda i, j: (l, k, j, 0)),
                pl.BlockSpec((1, d), lambda i, j: (0, 0)),
                pl.BlockSpec((1, d), lambda i, j: (0, 0))]
    args = [h, mod3, w_in_bf, w_in_bf, w_out_bf, ln_g.reshape(1, d), ln_b.reshape(1, d)]
    kern = functools.partial(_ffn_kernel, alpha=alpha)
    aliases = {}
    if out_buf is not None:
        in_specs.append(pl.BlockSpec(memory_space=pl.ANY))
        args.append(out_buf)
        aliases = {len(args) - 1: 0}
        kern = functools.partial(_ffn_kernel_inplace, alpha=alpha)
    return pl.pallas_call(
        kern,
        grid=(n_rows // tm, nf),
        in_specs=in_specs,
        out_specs=pl.BlockSpec((tm, d), lambda i, j: (i + tile0, 0)),
        out_shape=jax.ShapeDtypeStruct((out_rows, d), F32),
        input_output_aliases=aliases,
        compiler_params=_cparams(("arbitrary", "arbitrary"), V7X_VMEM_LIMIT_BIG),
        name="ffn",
    )(*args)


ROW_ALIGN = 16


def _gather_rows_kernel(starts_ref, src_ref, tail_ref, dst_ref, *, n_src_blocks):
    del starts_ref
    j = pl.program_id(1)

    @pl.when(j < n_src_blocks)
    def _():
        dst_ref[...] = src_ref[0]

    @pl.when(j >= n_src_blocks)
    def _():
        dst_ref[...] = tail_ref[...]


def _gather_rows(src, tail, segments, tr=512):
    depth, _, d = src.shape
    assert all((b - a) % tr == 0 and a % ROW_ALIGN == 0 for a, b in segments) and tail.shape[1] % tr == 0
    starts = [a + k for a, b in segments for k in range(0, b - a, tr)]
    n_src, n_tail = len(starts), tail.shape[1] // tr
    starts = jnp.asarray(starts + [0] * n_tail, jnp.int32)
    grid_spec = pltpu.PrefetchScalarGridSpec(
        num_scalar_prefetch=1,
        grid=(depth, n_src + n_tail),
        in_specs=[pl.BlockSpec((pl.Element(1), pl.Element(tr), pl.Element(d)),
                               lambda l, j, st: (l, pl.multiple_of(st[j], ROW_ALIGN), 0)),
                  pl.BlockSpec((None, tr, d), lambda l, j, st: (l, jnp.maximum(j - n_src, 0), 0))],
        out_specs=pl.BlockSpec((None, tr, d), lambda l, j, st: (l, j, 0)))
    return pl.pallas_call(
        functools.partial(_gather_rows_kernel, n_src_blocks=n_src),
        grid_spec=grid_spec,
        out_shape=jax.ShapeDtypeStruct((depth, (n_src + n_tail) * tr, d), src.dtype),
        compiler_params=_cparams(("arbitrary", "arbitrary")),
        name="gather_rows",
    )(starts, src, tail)


def _proj_kernel(h_ref, mod_ref, w_ref, o_ref, u_scr):
    @pl.when(pl.program_id(1) == 0)
    def _():
        m = mod_ref[0]
        u_scr[...] = (h_ref[...] * (1.0 + m[1:2]) + m[0:1]).astype(BF16)

    o_ref[...] = _dot_nt(u_scr[...], w_ref[...])


def _proj(h, mod3, group_of_tile, w_big, l, tm, tn):
    n_rows, d = h.shape
    n = w_big.shape[1]
    return pl.pallas_call(
        _proj_kernel,
        grid=(n_rows // tm, n // tn),
        in_specs=[pl.BlockSpec((tm, d), lambda i, j: (i, 0)),
                  pl.BlockSpec((1, 3, d), lambda i, j: (group_of_tile(i), 0, 0)),
                  pl.BlockSpec((None, tn, d), lambda i, j: (l, j, 0))],
        out_specs=pl.BlockSpec((tm, tn), lambda i, j: (i, j)),
        out_shape=jax.ShapeDtypeStruct((n_rows, n), F32),
        scratch_shapes=[pltpu.VMEM((tm, d), BF16)],
        compiler_params=_cparams(("arbitrary", "arbitrary")),
        name="in_proj",
    )(h, mod3, w_big)


def _conv_kernel(c_ref, p_ref, n_ref, w_ref, b_ref, o_ref, *, n_lat_blocks, blocks_per_seq):
    i = pl.program_id(0)
    tb = c_ref.shape[0]
    is_lat = i < n_lat_blocks
    in_seq = i % blocks_per_seq
    keep_prev = jnp.logical_and(is_lat, in_seq != 0)
    keep_next = jnp.logical_and(is_lat, in_seq != blocks_per_seq - 1)
    cur = c_ref[...]
    up = jnp.concatenate([jnp.where(keep_prev, p_ref[...], 0.0), cur[:tb - GRID_W]], axis=0)
    down = jnp.concatenate([cur[GRID_W:], jnp.where(keep_next, n_ref[...], 0.0)], axis=0)
    w = w_ref[...]
    vert = jnp.where(is_lat, 1.0, 0.0)
    w_up, w_mid, w_dn = w[0:3] * vert, w[3:6], w[6:9] * vert

    def column(dw):
        return w_up[dw:dw + 1] * up + w_mid[dw:dw + 1] * cur + w_dn[dw:dw + 1] * down

    r = lax.broadcasted_iota(jnp.int32, (tb, 1), 0)
    row_len = jnp.where(is_lat, GRID_W, tb)
    pos = r & (row_len - 1)
    acc = (column(1) + b_ref[...]
           + jnp.where(pos != 0, pltpu.roll(column(0), 1, 0), 0.0)
           + jnp.where(pos != row_len - 1, pltpu.roll(column(2), tb - 1, 0), 0.0))
    o_ref[...] = _silu(acc)


def _conv(p, col0_blocks, w9, bias, n_lat_rows, seq, tc):
    n_rows = p.shape[0]
    ch = w9.shape[1]
    tb = SCAN_BLOCK
    halo_per_blk = tb // GRID_W
    n_halo_rows = n_rows // GRID_W
    kern = functools.partial(_conv_kernel, n_lat_blocks=n_lat_rows // tb, blocks_per_seq=seq // tb)
    return pl.pallas_call(
        kern,
        grid=(n_rows // tb, ch // tc),
        in_specs=[pl.BlockSpec((tb, tc), lambda i, j: (i, col0_blocks + j)),
                  pl.BlockSpec((GRID_W, tc), lambda i, j: (jnp.maximum(i * halo_per_blk - 1, 0), col0_blocks + j)),
                  pl.BlockSpec((GRID_W, tc),
                               lambda i, j: (jnp.minimum((i + 1) * halo_per_blk, n_halo_rows - 1), col0_blocks + j)),
                  pl.BlockSpec((9, tc), lambda i, j: (0, j)),
                  pl.BlockSpec((1, tc), lambda i, j: (0, j))],
        out_specs=pl.BlockSpec((tb, tc), lambda i, j: (i, j)),
        out_shape=jax.ShapeDtypeStruct((n_rows, ch), F32),
        compiler_params=_cparams(("arbitrary", "arbitrary")),
        name="short_conv",
    )(p, p, p, w9, bias)


def _cumsum_op(tb, length, reverse):
    r = lax.broadcasted_iota(jnp.int32, (tb, tb), 0)
    c = lax.broadcasted_iota(jnp.int32, (tb, tb), 1)
    shift = length.bit_length() - 1
    same = (r >> shift) == (c >> shift)
    return jnp.logical_and(same, (c >= r) if reverse else (c <= r)).astype(BF16)


def _apply_op(op, x):
    hi = x.astype(BF16)
    r1 = x - hi.astype(F32)
    mid = r1.astype(BF16)
    lo = (r1 - mid.astype(F32)).astype(BF16)
    n = x.shape[1]
    out = _dot(op, jnp.concatenate([hi, mid, lo], axis=1))
    return out[:, 0:n] + out[:, n:2 * n] + out[:, 2 * n:3 * n]


def _causal(length, reverse):
    r = lax.broadcasted_iota(jnp.int32, (length, length), 0)
    c = lax.broadcasted_iota(jnp.int32, (length, length), 1)
    return (c >= r) if reverse else (c <= r)


def _chunk_order(tb, reverse):
    idx = list(range(tb // CHUNK))
    return idx[::-1] if reverse else idx


LANE = 128
LOG2_E = 1.4426950408889634


def _half_masks():
    lane_lo = lax.broadcasted_iota(jnp.int32, (1, LANE), 1) < LANE // 2
    sub_lo = lax.broadcasted_iota(jnp.int32, (LANE, 1), 0) < LANE // 2
    return lane_lo, sub_lo


def _ssd_block(xs_ref, bm_ref, cm_ref, sm_ref, bias, aneg, y_ref, row0, s_ref, op, lmask, *, d, reverse, heads, hdim):
    tb = xs_ref.shape[0]
    hpg = heads // SSD_GROUPS
    assert hdim == LANE // 2 and SSD_STATE == LANE // 2 and hpg % 2 == 0 and SSD_GROUPS % 2 == 0
    end = 0 if reverse else tb - 1
    lane_lo, sub_lo = _half_masks()
    dt = _softplus(sm_ref[...] + bias)
    acum2 = _apply_op(op, dt * aneg) * LOG2_E
    acum2_t = acum2.T
    src_t = (acum2 - jnp.log2(dt)).T
    w_t = jnp.exp2(acum2_t[:, end:end + 1] - src_t)
    e_end = jnp.exp2(acum2[end:end + 1, :])
    xs = xs_ref[...].astype(BF16)
    bm_t = bm_ref[...].T
    cm = cm_ref[...].astype(BF16)
    for gp in range(SSD_GROUPS // 2):
        cm_pair = cm[:, gp * LANE:(gp + 1) * LANE]
        bm_pair_t = bm_t[gp * LANE:(gp + 1) * LANE, :]
        for gi in range(2):
            g = 2 * gp + gi
            in_g = sub_lo if gi == 0 else jnp.logical_not(sub_lo)
            cb = _dot(cm_pair, jnp.where(in_g, bm_pair_t, 0.0).astype(BF16))
            bmg_t = bm_t[g * SSD_STATE:(g + 1) * SSD_STATE, :]
            s_g = s_ref[g]
            zero = jnp.zeros_like(s_g)
            s_rows = jnp.concatenate([s_g, zero] if gi == 0 else [zero, s_g], axis=0).astype(BF16)
            y_in = _dot(cm_pair, s_rows)
            for pi in range(hpg // 2):
                gl = slice(pi * LANE, (pi + 1) * LANE)
                xl = slice(g * hpg * hdim + pi * LANE, g * hpg * hdim + (pi + 1) * LANE)
                xs_pair = xs[:, xl]
                xs_half = (jnp.where(lane_lo, xs_pair, 0.0).astype(BF16), jnp.where(lane_lo, 0.0, xs_pair).astype(BF16))
                c0 = d * heads + g * hpg + 2 * pi
                y_acc = None
                s_acc = None
                e_col = []
                for hi in range(2):
                    c = c0 + hi
                    colb = jnp.broadcast_to(acum2[:, c:c + 1], (tb, tb))
                    m = cb * jnp.exp2(jnp.where(lmask, colb - src_t[c:c + 1, :], -jnp.inf))
                    y_h = _dot(m.astype(BF16), xs_half[hi])
                    s_h = _dot((bmg_t * w_t[c:c + 1, :]).astype(BF16), xs_half[hi])
                    y_acc = y_h if y_acc is None else y_acc + y_h
                    s_acc = s_h if s_acc is None else s_acc + s_h
                    e_col.append(jnp.exp2(colb[:, 0:LANE]))
                y_ref[row0:row0 + tb, xl] = y_acc + y_in[:, gl] * jnp.where(lane_lo, e_col[0], e_col[1])
                e_pair = jnp.where(lane_lo, e_end[:, c0:c0 + 1], e_end[:, c0 + 1:c0 + 2])
                s_ref[g, :, gl] = s_g[:, gl] * e_pair + s_acc
                yield


def _ssd_kernel(*refs, batch, d, reverse, heads, hdim):
    ins, (bias_ref, aneg_ref, y_ref, s_scr) = refs[:4 * batch], refs[4 * batch:]

    @pl.when(pl.program_id(0) == 0)
    def _():
        s_scr[...] = jnp.zeros_like(s_scr)

    tb = SCAN_BLOCK
    op = _cumsum_op(tb, tb, reverse)
    lmask = _causal(tb, reverse)
    _interleave(_ssd_block(*ins[4 * b:4 * b + 4], bias_ref[...], aneg_ref[...], y_ref, b * tb, s_scr.at[b], op, lmask,
                           d=d, reverse=reverse, heads=heads, hdim=hdim) for b in range(batch))


def _ml_kernel(*refs, batch, d, reverse, heads, gate_col0):
    ins, (gbi_ref, gbf_ref, y_ref, ct_scr, m_scr) = refs[:5 * batch], refs[5 * batch:]

    @pl.when(pl.program_id(0) == 0)
    def _():
        ct_scr[...] = jnp.zeros_like(ct_scr)
        m_scr[...] = jnp.zeros_like(m_scr)

    tb = SCAN_BLOCK
    op = _cumsum_op(tb, tb, reverse)
    lmask = _causal(tb, reverse)
    states = [([ct_scr[b, h] for h in range(heads)], [m_scr[b, h][:, 0:1] for h in range(heads)]) for b in range(batch)]
    new = [([], []) for _ in range(batch)]
    _interleave(_ml_block(*ins[5 * b:5 * b + 5], gbi_ref[...], gbf_ref[...], y_ref, b * tb, states[b][0], states[b][1],
                          new[b][0], new[b][1], op, lmask, d=d, reverse=reverse, heads=heads, gate_col0=gate_col0)
                for b in range(batch))
    for b in range(batch):
        for h in range(heads):
            ct_scr[b, h] = new[b][0][h]
            m_scr[b, h] = jnp.broadcast_to(new[b][1][h], m_scr.shape[2:])


def _cummax_rows(x, reverse):
    n = x.shape[0]
    row = lax.broadcasted_iota(jnp.int32, (n, 1), 0)
    s = 1
    while s < n:
        if s < 8:
            if reverse:
                shifted = jnp.where(row < n - s, pltpu.roll(x, n - s, 0), -jnp.inf)
            else:
                shifted = jnp.where(row >= s, pltpu.roll(x, s, 0), -jnp.inf)
        else:
            pad = jnp.full((s, x.shape[1]), -jnp.inf, F32)
            shifted = jnp.concatenate([x[s:], pad] if reverse else [pad, x[:n - s]], axis=0)
        x = jnp.maximum(x, shifted)
        s *= 2
    return x


def _ml_block(q_ref, k_ref, v_ref, smi_ref, smf_ref, gate_bi, gate_bf, y_ref, row0, ct_in, m_in, ct_out, m_out, op, lmask,
              *, d, reverse, heads, gate_col0):
    tb = q_ref.shape[0]
    hd = q_ref.shape[1] // heads
    assert hd == LANE
    end = 0 if reverse else tb - 1
    base = gate_col0 + d * 2 * heads
    raw = smi_ref[...] + gate_bi
    bcum = _apply_op(op, _log_sigmoid(smf_ref[...] + gate_bf))
    r = raw - bcum
    r_t = r.T
    cmax = _cummax_rows(r, reverse)
    q_all = (q_ref[...] * (hd ** -0.5)).astype(BF16)
    k_all = k_ref[...].astype(BF16)
    v_all = v_ref[...].astype(BF16)
    ones = jnp.ones((tb, hd), BF16)
    yield
    for h in range(heads):
        c = base + h
        hs = slice(h * hd, (h + 1) * hd)
        qb = q_all[:, hs]
        k_t = k_all[:, hs].T
        v_ext = jnp.concatenate([v_all[:, hs], ones], axis=1)
        m_st = m_in[h]
        ct = ct_in[h]
        r_row = r_t[c:c + 1, :]
        b_rep = jnp.broadcast_to(bcum[:, c:c + 1], (tb, hd))
        u_rep = jnp.broadcast_to(jnp.maximum(cmax[:, c:c + 1], m_st), (tb, hd))
        s = jnp.exp(jnp.where(lmask, r_row - jnp.concatenate([u_rep, u_rep], axis=1), -jnp.inf)) * _dot(qb, k_t)
        g_rep = jnp.exp(m_st - u_rep)
        acc = _dot(s.astype(BF16), v_ext) + jnp.concatenate([g_rep, g_rep], axis=1) * _dot(qb, ct.astype(BF16))
        y_ref[row0:row0 + tb, hs] = acc[:, 0:hd] / jnp.maximum(jnp.abs(acc[:, hd:]), jnp.exp(-(b_rep + u_rep)))
        b_end = bcum[end:end + 1, c:c + 1]
        a_row = b_end + r_row
        m_loc = jnp.max(a_row, axis=1, keepdims=True)
        upd = _dot((k_t * jnp.exp(a_row - m_loc)).astype(BF16), v_ext)
        m_new = jnp.maximum(b_end + m_st, m_loc)
        ct_out.append(jnp.exp(b_end + m_st - m_new) * ct + jnp.exp(m_loc - m_new) * upd)
        m_out.append(m_new)
        yield


def _gla_block(q_ref, k_ref, v_ref, sm_ref, w2, b2, y_ref, row0, s_scr, op, lmask, *, reverse, heads):
    tb = q_ref.shape[0]
    dk = q_ref.shape[1] // heads
    dv = v_ref.shape[1] // heads
    assert dk == LANE // 2 and heads % 2 == 0
    lane_lo, _ = _half_masks()
    x = _dot(sm_ref[...].astype(BF16), w2) + b2
    b = _apply_op(op, _log_sigmoid(x) * (1.0 / GLA_TAU))
    q_d = (q_ref[...] * (dk ** -0.5) * jnp.exp(b)).astype(BF16)
    k = k_ref[...]
    k_d = k * jnp.exp(-b)
    v = v_ref[...].astype(BF16)
    state = [s_scr[h] for h in range(heads)]
    for ci in _chunk_order(tb, reverse):
        lo = ci * CHUNK
        sl = slice(lo, lo + CHUNK)
        end = lo if reverse else lo + CHUNK - 1
        b_end = b[end:end + 1, :]
        kk = k[sl] * jnp.exp(b_end - b[sl])
        e_end = jnp.exp(b_end)
        att, inter = [], []
        for h in range(heads):
            pl_ = slice(h // 2 * LANE, (h // 2 + 1) * LANE)
            mine = lane_lo if h % 2 == 0 else jnp.logical_not(lane_lo)
            vs = slice(h * dv, (h + 1) * dv)
            q_pair = q_d[sl, pl_]
            k_m = jnp.where(mine, k_d[sl, pl_], 0.0).astype(BF16)
            kk_m = jnp.where(mine, kk[:, pl_], 0.0).astype(BF16)
            att.append(_dot_nt(q_pair, k_m))
            inter.append(_dot_nt(q_pair, state[h].astype(BF16)))
            state[h] = state[h] * e_end[:, pl_] + _dot_tn(v[sl, vs], kk_m)
        yield
        outs = [_dot(jnp.where(lmask, att[h], 0.0).astype(BF16), v[sl, h * dv:(h + 1) * dv]) + inter[h]
                for h in range(heads)]
        y_ref[row0 + lo:row0 + lo + CHUNK, :] = jnp.concatenate(outs, axis=1)
        yield
    for h in range(heads):
        s_scr[h] = state[h]


def _interleave(gens):
    gens = list(gens)
    while gens:
        alive = []
        for g in gens:
            try:
                next(g)
                alive.append(g)
            except StopIteration:
                pass
        gens = alive


def _gla_kernel(*refs, batch, reverse, heads):
    ins, (w2_ref, b2_ref, y_ref, s_scr) = refs[:4 * batch], refs[4 * batch:]

    @pl.when(pl.program_id(0) == 0)
    def _():
        s_scr[...] = jnp.zeros_like(s_scr)

    tb = SCAN_BLOCK
    op = _cumsum_op(tb, CHUNK, reverse)
    lmask = _causal(CHUNK, reverse)
    _interleave(_gla_block(*ins[4 * b:4 * b + 4], w2_ref[...], b2_ref[...], y_ref, b * tb, s_scr.at[b], op, lmask,
                           reverse=reverse, heads=heads) for b in range(batch))


def _scan_rows(batch, seq, ctx_len, reverse):
    tb = SCAN_BLOCK
    nc, nl = ctx_len // tb, seq // tb
    ctx_base = batch * seq // tb

    def rows(b, s):
        if reverse:
            return jnp.where(s < nc, ctx_base + b * nc + (nc - 1 - s), b * nl + (nl - 1 - (s - nc)))
        return jnp.where(s < nc, ctx_base + b * nc + s, b * nl + (s - nc))

    return rows, nc + nl


def _scan_out_rows(batch, seq, ctx_len, reverse, tile):
    tb = SCAN_BLOCK
    nc, nl = ctx_len // tb, seq // tb
    sub = tb // tile

    def rows(i):
        q, part = i // sub, i % sub
        is_lat = q < batch * nl
        qc = jnp.maximum(q - batch * nl, 0)
        b = jnp.where(is_lat, q // nl, qc // nc)
        k = jnp.where(is_lat, q % nl, qc % nc)
        n = jnp.where(is_lat, nl, nc)
        s = jnp.where(is_lat, nc, 0) + (n - 1 - k if reverse else k)
        return (s * batch + b) * sub + part

    return rows


def _scan_call(kern, name, batch, rows, steps, ins, consts, out_w, scratch):
    tb = SCAN_BLOCK
    in_specs = [pl.BlockSpec((tb, w), functools.partial(lambda s, b, cb: (rows(b, s), cb), b=b, cb=cb))
                for b in range(batch) for _, w, cb in ins]
    in_specs += [pl.BlockSpec(c.shape, lambda s: (0, 0)) for c in consts]
    return pl.pallas_call(
        kern,
        grid=(steps,),
        in_specs=in_specs,
        out_specs=pl.BlockSpec((batch * tb, out_w), lambda s: (s, 0)),
        out_shape=jax.ShapeDtypeStruct((steps * batch * tb, out_w), F32),
        scratch_shapes=scratch,
        compiler_params=_cparams(("arbitrary",)),
        name=name,
    )(*[a for _ in range(batch) for a, _, _ in ins], *consts)


def _group_norm(y, groups, center):
    gw = y.shape[1] // groups
    outs = []
    for g in range(groups):
        yg = y[:, g * gw:(g + 1) * gw]
        if center:
            yg = yg - jnp.mean(yg, axis=-1, keepdims=True)
        outs.append(yg * lax.rsqrt(jnp.mean(yg * yg, axis=-1, keepdims=True) + EPS))
    return jnp.concatenate(outs, axis=1)


def _merge_kernel(h_ref, mod_ref, gate_ref, z_ref, xs_ref, ysf_ref, ysb_ref, o_ref, ymf_ref, ymb_ref,
                  r_ref, ygf_ref, ygb_ref, dskip_ref, snw_ref, mnw_ref, gnw_ref, mb_ref,
                  wbs_ref, wbm_ref, wbg_ref, wo_ref, lng_ref, lnb_ref, out_ref, *, alpha, ml_heads):
    d = h_ref.shape[1]
    ys = (ysf_ref[...] + ysb_ref[...] + dskip_ref[...] * xs_ref[...]) * _silu(z_ref[...])
    ys = _group_norm(ys, SSD_GROUPS, False) * snw_ref[...]
    b_ssd = _dot(ys.astype(BF16), wbs_ref[...])
    ym = _sigmoid(o_ref[...]) * (ymf_ref[...] + ymb_ref[...])
    ym = _group_norm(ym, ml_heads, True) * mnw_ref[...]
    b_ml = _dot(ym.astype(BF16), wbm_ref[...])
    yg = _group_norm(ygf_ref[...] + ygb_ref[...], GLA_HEADS, True) * gnw_ref[...] * _silu(r_ref[...])
    b_gla = _dot(yg.astype(BF16), wbg_ref[...])
    mb = mb_ref[...]
    mix = (_sigmoid(gate_ref[:, 0:d] + mb[0:1]) * b_ssd
           + _sigmoid(gate_ref[:, d:2 * d] + mb[1:2]) * b_ml
           + _sigmoid(gate_ref[:, 2 * d:3 * d] + mb[2:3]) * b_gla)
    y = _dot(mix.astype(BF16), wo_ref[...])
    m = mod_ref[0]
    out_ref[...] = _layer_norm(alpha * h_ref[...] + m[2:3] * y, lng_ref[...], lnb_ref[...])


def kernel(x, c, ctx, c_ctx, w_mod, b_mod, ln_g, ln_b, ffn_w_in, ffn_w_out, w_in, merge_b, ssd_conv_w, ssd_conv_b, ssd_dt_bias, ssd_a_log, ssd_d, ssd_norm_w, ml_conv_w, ml_conv_b, ml_gate_b, ml_norm_w, gla_w2, gla_b2, gla_norm_w, w_br_ssd, w_br_ml, w_br_gla, w_out):
    batch, seq, d = x.shape
    ctx_len = ctx.shape[1]
    depth = w_mod.shape[0]
    alpha = (2 * depth) ** 0.25
    ssd_inner = ssd_norm_w.shape[1]
    ssd_heads = ssd_dt_bias.shape[2]
    ssd_hdim = ssd_inner // ssd_heads
    bc_w = 2 * SSD_GROUPS * SSD_STATE
    ml_w = ml_norm_w.shape[1]
    ml_heads = ml_gate_b.shape[3]
    gla_w = gla_norm_w.shape[1]
    gla_kw = gla_w2.shape[3]
    gla_rank = gla_w2.shape[2]
    n_lat = batch * seq
    n_all = n_lat + batch * ctx_len
    tb = SCAN_BLOCK
    assert seq % tb == 0 and ctx_len == tb
    assert 2 * ssd_heads + 4 * ml_heads + 2 * gla_rank <= SMALL_W

    tm = next(t for t in (1024, 512, tb) if seq % t == 0 and (batch * ctx_len) % t == 0)
    tm_merge = 256
    tf = 512

    def group_of_tile_for(t):
        per_batch = seq // t
        return lambda i: jnp.minimum(i // per_batch, batch)

    assert (d, ssd_inner, ml_w, gla_w, gla_kw, bc_w) == (2048, 1024, 512, 512, 256, 512)
    gate_w = 3 * d
    col_z = gate_w
    col_mlv = col_z + ssd_inner
    col_mlqk = col_mlv + ml_w
    col_xs = col_mlqk + 2 * ml_w
    col_bc = col_xs + ssd_inner
    col_mlo = col_bc + bc_w
    col_glav = col_mlo + ml_w
    col_glar = col_glav + gla_w
    col_glaq = col_glar + gla_w
    col_glak = col_glaq + gla_kw
    col_small = col_glak + gla_kw
    tn_proj = 1280
    n_proj = -(-(col_small + 2 * SMALL_W) // tn_proj) * tn_proj
    conv_w = 2 * ml_w + ssd_inner + bc_w
    tc_conv = conv_w // 2
    assert col_mlqk % tc_conv == 0

    in_sizes = (ssd_inner, ssd_inner + bc_w, 2 * ssd_heads, ml_w, ml_w, ml_w, ml_w, 4 * ml_heads,
                gla_kw, gla_kw, gla_w, gla_w, 2 * gla_rank, gate_w)
    offs = [0]
    for sz in in_sizes:
        offs.append(offs[-1] + sz)
    (o_z, o_xbc, o_dt, o_mlq, o_mlk, o_mlv, o_mlo, o_mlg, o_gq, o_gk, o_gv, o_gr, o_lr, o_gate, o_end) = offs

    rows_mod = 8
    cpad = jnp.concatenate([c, c_ctx[None], jnp.zeros((rows_mod - batch - 1, d), F32)], axis=0)
    mod = _modulation(cpad, w_mod, b_mod).reshape(depth, rows_mod, N_MOD, d)

    ffn_in_bf = ffn_w_in.astype(BF16)
    ffn_out_bf = ffn_w_out.astype(BF16)
    w_br_bf = [w.astype(BF16) for w in (w_br_ssd, w_br_ml, w_br_gla)]
    w_out_bf = w_out.astype(BF16)
    g0 = 2 * ssd_heads
    small_used = 2 * ssd_heads + 4 * ml_heads + 2 * gla_rank
    fgate = []
    for dd in range(2):
        f0 = o_mlg + (2 * dd + 1) * ml_heads
        fgate += [(f0, f0 + ml_heads), ml_heads]
    w_t = jnp.swapaxes(w_in, 1, 2).astype(BF16)
    small_segs = ([(o_dt, o_mlq), (o_mlg, o_gq), (o_lr, o_gate), SMALL_W - small_used, g0] + fgate
                  + [n_proj - col_small - SMALL_W - g0 - 4 * ml_heads])
    tail = jnp.concatenate([jnp.zeros((depth, seg, d), BF16) if isinstance(seg, int) else w_t[:, seg[0]:seg[1]]
                            for seg in small_segs], axis=1)
    wide_segs = [(o_gate, o_end), (o_z, o_xbc), (o_mlv, o_mlo), (o_mlq, o_mlv), (o_xbc, o_dt), (o_mlo, o_mlg),
                 (o_gv, o_lr), (o_gq, o_gv)]
    w_big = _gather_rows(w_t, tail, wide_segs)
    assert w_big.shape[1] == n_proj

    h = None
    for l in range(depth):
        last = l == depth - 1
        n_out = n_lat if last else n_all

        conv_w9 = jnp.concatenate([ml_conv_w[l].reshape(9, 2 * ml_w), ssd_conv_w[l].reshape(9, ssd_inner + bc_w)], axis=1)
        conv_b = jnp.concatenate([ml_conv_b[l], ssd_conv_b[l]])[None]

        def small_row(vals, start):
            return jnp.zeros((1, SMALL_W), F32).at[0, start:start + vals.shape[0]].set(vals)

        dt_bias_row = small_row(ssd_dt_bias[l].reshape(-1), 0)
        a_neg_row = small_row(-jnp.exp(ssd_a_log[l].astype(F32)).reshape(-1), 0)
        gate_bi_row = small_row(ml_gate_b[l].reshape(-1), g0)
        gate_bf_row = small_row(jnp.concatenate([ml_gate_b[l, :, 1], jnp.zeros((2, ml_heads), F32)], axis=1).reshape(-1), g0)
        lr0 = 2 * ssd_heads + 4 * ml_heads
        w2_pad = [jnp.zeros((SMALL_W, gla_kw), F32).at[lr0 + dd * gla_rank:lr0 + (dd + 1) * gla_rank].set(gla_w2[l, dd])
                  for dd in range(2)]

        ffn0 = functools.partial(_ffn, mod3=mod[l, :, 0:3], group_of_tile=group_of_tile_for(tm), w_in_bf=ffn_in_bf,
                                 w_out_bf=ffn_out_bf, lk=(l, 0), ln_g=ln_g[l, 0], ln_b=ln_b[l, 0], alpha=alpha,
                                 tm=tm, tf=tf)
        if h is None:
            h = ffn0(x.reshape(n_lat, d), n_lat, out_rows=n_all)
            h = ffn0(ctx.reshape(batch * ctx_len, d), batch * ctx_len, out_rows=n_all, tile0=n_lat // tm, out_buf=h)
        else:
            h = ffn0(h, n_all)

        p = _proj(h, mod[l, :, 3:6], group_of_tile_for(tm), w_big, l, tm, tn_proj)
        cv = _conv(p, col_mlqk // tc_conv, conv_w9, conv_b, n_lat, seq, tc_conv)
        col_cvb = 2 * ml_w + ssd_inner
        ys, ym, yg = [], [], []
        for dd, reverse in ((0, False), (1, True)):
            rows, steps = _scan_rows(batch, seq, ctx_len, reverse)
            ys.append(_scan_call(
                functools.partial(_ssd_kernel, batch=batch, d=dd, reverse=reverse, heads=ssd_heads, hdim=ssd_hdim),
                "ssd_scan", batch, rows, steps,
                [(cv, ssd_inner, 2 * ml_w // ssd_inner), (cv, bc_w // 2, col_cvb // (bc_w // 2)),
                 (cv, bc_w // 2, col_cvb // (bc_w // 2) + 1), (p, SMALL_W, col_small // SMALL_W)],
                [dt_bias_row, a_neg_row], ssd_inner,
                [pltpu.VMEM((batch, SSD_GROUPS, SSD_STATE, ssd_inner // SSD_GROUPS), F32)]))
            ym.append(_scan_call(
                functools.partial(_ml_kernel, batch=batch, d=dd, reverse=reverse, heads=ml_heads,
                                  gate_col0=2 * ssd_heads),
                "mlstm_scan", batch, rows, steps,
                [(cv, ml_w, 0), (cv, ml_w, 1), (p, ml_w, col_mlv // ml_w), (p, SMALL_W, col_small // SMALL_W),
                 (p, SMALL_W, col_small // SMALL_W + 1)],
                [gate_bi_row, gate_bf_row], ml_w,
                [pltpu.VMEM((batch, ml_heads, ml_w // ml_heads, 2 * ml_w // ml_heads), F32),
                 pltpu.VMEM((batch, ml_heads, 1, ml_w // ml_heads), F32)]))
            yg.append(_scan_call(
                functools.partial(_gla_kernel, batch=batch, reverse=reverse, heads=GLA_HEADS),
                "gla_scan", batch, rows, steps,
                [(p, gla_kw, col_glaq // gla_kw), (p, gla_kw, col_glak // gla_kw), (p, gla_w, col_glav // gla_w),
                 (p, SMALL_W, col_small // SMALL_W)],
                [w2_pad[dd].astype(BF16), gla_b2[l, dd][None]], gla_w,
                [pltpu.VMEM((batch, GLA_HEADS, gla_w // GLA_HEADS, LANE), F32)]))

        tmm = tm_merge
        row = lambda i: (i, 0)
        const = lambda i: (0, 0)
        y_fwd = _scan_out_rows(batch, seq, ctx_len, False, tmm)
        y_bwd = _scan_out_rows(batch, seq, ctx_len, True, tmm)
        row_f = lambda i: (y_fwd(i), 0)
        row_b = lambda i: (y_bwd(i), 0)

        def col(cb):
            return lambda i: (i, cb)

        g_of = group_of_tile_for(tmm)
        consts = [jnp.repeat(ssd_d[l], ssd_hdim)[None], ssd_norm_w[l][None], ml_norm_w[l][None], gla_norm_w[l][None],
                  merge_b[l]]
        weights = [w_br_bf[0], w_br_bf[1], w_br_bf[2], w_out_bf]
        ln_consts = [ln_g[l, 1][None], ln_b[l, 1][None]]
        h_mix = pl.pallas_call(
            functools.partial(_merge_kernel, alpha=alpha, ml_heads=ml_heads),
            grid=(n_out // tmm,),
            in_specs=[pl.BlockSpec((tmm, d), row),
                      pl.BlockSpec((1, 3, d), lambda i: (g_of(i), 0, 0)),
                      pl.BlockSpec((tmm, gate_w), col(0)),
                      pl.BlockSpec((tmm, ssd_inner), col(col_z // ssd_inner)),
                      pl.BlockSpec((tmm, ssd_inner), col(2 * ml_w // ssd_inner)),
                      pl.BlockSpec((tmm, ssd_inner), row_f),
                      pl.BlockSpec((tmm, ssd_inner), row_b),
                      pl.BlockSpec((tmm, ml_w), col(col_mlo // ml_w)),
                      pl.BlockSpec((tmm, ml_w), row_f),
                      pl.BlockSpec((tmm, ml_w), row_b),
                      pl.BlockSpec((tmm, gla_w), col(col_glar // gla_w)),
                      pl.BlockSpec((tmm, gla_w), row_f),
                      pl.BlockSpec((tmm, gla_w), row_b)]
                     + [pl.BlockSpec(a.shape, const) for a in consts]
                     + [pl.BlockSpec((None,) + a.shape[1:], functools.partial(lambda i, l: (l, 0, 0), l=l),
                                     pipeline_mode=pl.Buffered(1))
                        for a in weights]
                     + [pl.BlockSpec(a.shape, const) for a in ln_consts],
            out_specs=pl.BlockSpec((tmm, d), row),
            out_shape=jax.ShapeDtypeStruct((n_out, d), F32),
            compiler_params=_cparams(("arbitrary",)),
            name="merge_out",
        )(h, mod[l, :, 3:6], p, p, cv, ys[0], ys[1], p, ym[0], ym[1], p, yg[0], yg[1], *consts, *weights, *ln_consts)

        h = _ffn(h_mix, n_out, mod[l, :, 6:9], group_of_tile_for(tm), ffn_in_bf, ffn_out_bf, (l, 1), ln_g[l, 2],
                 ln_b[l, 2], alpha, tm, tf)

    return h.reshape(batch, seq, d)
```

```python
import functools

import jax
import jax.numpy as jnp
from jax import lax
from jax.experimental import pallas as pl
from jax.experimental.pallas import tpu as pltpu

F32 = jnp.float32
BF16 = jnp.bfloat16

GRID_W = 64
CHUNK = 64
SSD_GROUPS = 4
SSD_STATE = 64
GLA_HEADS = 4
GLA_TAU = 16.0
EPS = 1e-5
N_MOD = 9

SCAN_BLOCK = 256
SMALL_W = 128
V7X_VMEM_LIMIT = 56 * 1024 * 1024
V7X_VMEM_LIMIT_BIG = 60 * 1024 * 1024

NT_DIMS = (((1,), (1,)), ((), ()))
TN_DIMS = (((0,), (0,)), ((), ()))


def _cparams(sem, vmem=V7X_VMEM_LIMIT):
    return pltpu.CompilerParams(dimension_semantics=sem, vmem_limit_bytes=vmem)


def _dot(a, b):
    return jnp.dot(a, b, preferred_element_type=F32)


def _dot_nt(a, b):
    return lax.dot_general(a, b, NT_DIMS, preferred_element_type=F32)


def _dot_tn(a, b):
    return lax.dot_general(a, b, TN_DIMS, preferred_element_type=F32)


def _sigmoid(x):
    return 1.0 / (1.0 + jnp.exp(-x))


def _silu(x):
    return x * _sigmoid(x)


def _softplus(x):
    return jnp.maximum(x, 0.0) + jnp.log(1.0 + jnp.exp(-jnp.abs(x)))


def _log_sigmoid(x):
    return -_softplus(-x)


def _layer_norm(x, g, b):
    mu = jnp.mean(x, axis=-1, keepdims=True)
    xc = x - mu
    var = jnp.mean(xc * xc, axis=-1, keepdims=True)
    return xc * lax.rsqrt(var + EPS) * g + b


def _mod_kernel(c_ref, w_ref, b_ref, o_ref):
    c = c_ref[...]
    o_ref[0] = _dot(_silu(c).astype(BF16), w_ref[0].astype(BF16)) + b_ref[0]


def _modulation(cpad, w_mod, b_mod, tn=1152):
    depth, d, n = w_mod.shape
    rows = cpad.shape[0]
    return pl.pallas_call(
        _mod_kernel,
        grid=(depth, n // tn),
        in_specs=[pl.BlockSpec((rows, d), lambda l, j: (0, 0)),
                  pl.BlockSpec((1, d, tn), lambda l, j: (l, 0, j)),
                  pl.BlockSpec((1, 1, tn), lambda l, j: (l, 0, j))],
        out_specs=pl.BlockSpec((1, rows, tn), lambda l, j: (l, 0, j)),
        out_shape=jax.ShapeDtypeStruct((depth, rows, n), F32),
        compiler_params=_cparams(("arbitrary", "arbitrary")),
        name="modulation",
    )(cpad, w_mod, b_mod.reshape(depth, 1, n))


def _ffn_kernel(h_ref, mod_ref, wa_ref, wg_ref, wo_ref, lng_ref, lnb_ref, o_ref, *, alpha):
    j = pl.program_id(1)
    m = mod_ref[0]
    u = (h_ref[...] * (1.0 + m[1:2]) + m[0:1]).astype(BF16)
    a = _dot(u, wa_ref[...])
    g = _dot(u, wg_ref[...])

    @pl.when(j == 0)
    def _():
        o_ref[...] = jnp.zeros_like(o_ref)

    o_ref[...] += _dot((_silu(g) * a).astype(BF16), wo_ref[...])

    @pl.when(j == pl.num_programs(1) - 1)
    def _():
        x = alpha * h_ref[...] + m[2:3] * (0.5 * o_ref[...])
        o_ref[...] = _layer_norm(x, lng_ref[...], lnb_ref[...])


def _ffn_kernel_inplace(h_ref, mod_ref, wa_ref, wg_ref, wo_ref, lng_ref, lnb_ref, buf_ref, o_ref, *, alpha):
    del buf_ref
    _ffn_kernel(h_ref, mod_ref, wa_ref, wg_ref, wo_ref, lng_ref, lnb_ref, o_ref, alpha=alpha)


def _ffn(h, n_rows, mod3, group_of_tile, w_in_bf, w_out_bf, lk, ln_g, ln_b, alpha, tm, tf, out_rows=None, tile0=0,
         out_buf=None):
    d = h.shape[1]
    dff = w_out_bf.shape[2]
    nf = dff // tf
    l, k = lk
    out_rows = n_rows if out_rows is None else out_rows
    in_specs = [pl.BlockSpec((tm, d), lambda i, j: (i, 0)),
                pl.BlockSpec((1, 3, d), lambda i, j: (group_of_tile(i + tile0), 0, 0)),
                pl.BlockSpec((None, None, d, tf), lambda i, j: (l, k, 0, j)),
                pl.BlockSpec((None, None, d, tf), lambda i, j: (l, k, 0, j + nf)),
                pl.BlockSpec((None, None, tf, d), lambda i, j: (l, k, j, 0)),
                pl.BlockSpec((1, d), lambda i, j: (0, 0)),
                pl.BlockSpec((1, d), lambda i, j: (0, 0))]
    args = [h, mod3, w_in_bf, w_in_bf, w_out_bf, ln_g.reshape(1, d), ln_b.reshape(1, d)]
    kern = functools.partial(_ffn_kernel, alpha=alpha)
    aliases = {}
    if out_buf is not None:
        in_specs.append(pl.BlockSpec(memory_space=pl.ANY))
        args.append(out_buf)
        aliases = {len(args) - 1: 0}
        kern = functools.partial(_ffn_kernel_inplace, alpha=alpha)
    return pl.pallas_call(
        kern,
        grid=(n_rows // tm, nf),
        in_specs=in_specs,
        out_specs=pl.BlockSpec((tm, d), lambda i, j: (i + tile0, 0)),
        out_shape=jax.ShapeDtypeStruct((out_rows, d), F32),
        input_output_aliases=aliases,
        compiler_params=_cparams(("arbitrary", "arbitrary"), V7X_VMEM_LIMIT_BIG),
        name="ffn",
    )(*args)


ROW_ALIGN = 16


def _gather_rows_kernel(starts_ref, src_ref, tail_ref, dst_ref, *, n_src_blocks):
    del starts_ref
    j = pl.program_id(1)

    @pl.when(j < n_src_blocks)
    def _():
        dst_ref[...] = src_ref[0]

    @pl.when(j >= n_src_blocks)
    def _():
        dst_ref[...] = tail_ref[...]


def _gather_rows(src, tail, segments, tr=512):
    depth, _, d = src.shape
    assert all((b - a) % tr == 0 and a % ROW_ALIGN == 0 for a, b in segments) and tail.shape[1] % tr == 0
    starts = [a + k for a, b in segments for k in range(0, b - a, tr)]
    n_src, n_tail = len(starts), tail.shape[1] // tr
    starts = jnp.asarray(starts + [0] * n_tail, jnp.int32)
    grid_spec = pltpu.PrefetchScalarGridSpec(
        num_scalar_prefetch=1,
        grid=(depth, n_src + n_tail),
        in_specs=[pl.BlockSpec((pl.Element(1), pl.Element(tr), pl.Element(d)),
                               lambda l, j, st: (l, pl.multiple_of(st[j], ROW_ALIGN), 0)),
                  pl.BlockSpec((None, tr, d), lambda l, j, st: (l, jnp.maximum(j - n_src, 0), 0))],
        out_specs=pl.BlockSpec((None, tr, d), lambda l, j, st: (l, j, 0)))
    return pl.pallas_call(
        functools.partial(_gather_rows_kernel, n_src_blocks=n_src),
        grid_spec=grid_spec,
        out_shape=jax.ShapeDtypeStruct((depth, (n_src + n_tail) * tr, d), src.dtype),
        compiler_params=_cparams(("arbitrary", "arbitrary")),
        name="gather_rows",
    )(starts, src, tail)


def _proj_kernel(h_ref, mod_ref, w_ref, o_ref, u_scr):
    @pl.when(pl.program_id(1) == 0)
    def _():
        m = mod_ref[0]
        u_scr[...] = (h_ref[...] * (1.0 + m[1:2]) + m[0:1]).astype(BF16)

    o_ref[...] = _dot_nt(u_scr[...], w_ref[...])


def _proj(h, mod3, group_of_tile, w_big, l, tm, tn):
    n_rows, d = h.shape
    n = w_big.shape[1]
    return pl.pallas_call(
        _proj_kernel,
        grid=(n_rows // tm, n // tn),
        in_specs=[pl.BlockSpec((tm, d), lambda i, j: (i, 0)),
                  pl.BlockSpec((1, 3, d), lambda i, j: (group_of_tile(i), 0, 0)),
                  pl.BlockSpec((None, tn, d), lambda i, j: (l, j, 0))],
        out_specs=pl.BlockSpec((tm, tn), lambda i, j: (i, j)),
        out_shape=jax.ShapeDtypeStruct((n_rows, n), F32),
        scratch_shapes=[pltpu.VMEM((tm, d), BF16)],
        compiler_params=_cparams(("arbitrary", "arbitrary")),
        name="in_proj",
    )(h, mod3, w_big)


def _conv_kernel(c_ref, p_ref, n_ref, w_ref, b_ref, o_ref, *, n_lat_blocks, blocks_per_seq):
    i = pl.program_id(0)
    tb = c_ref.shape[0]
    is_lat = i < n_lat_blocks
    in_seq = i % blocks_per_seq
    keep_prev = jnp.logical_and(is_lat, in_seq != 0)
    keep_next = jnp.logical_and(is_lat, in_seq != blocks_per_seq - 1)
    cur = c_ref[...]
    up = jnp.concatenate([jnp.where(keep_prev, p_ref[...], 0.0), cur[:tb - GRID_W]], axis=0)
    down = jnp.concatenate([cur[GRID_W:], jnp.where(keep_next, n_ref[...], 0.0)], axis=0)
    w = w_ref[...]
    vert = jnp.where(is_lat, 1.0, 0.0)
    w_up, w_mid, w_dn = w[0:3] * vert, w[3:6], w[6:9] * vert

    def column(dw):
        return w_up[dw:dw + 1] * up + w_mid[dw:dw + 1] * cur + w_dn[dw:dw + 1] * down

    r = lax.broadcasted_iota(jnp.int32, (tb, 1), 0)
    row_len = jnp.where(is_lat, GRID_W, tb)
    pos = r & (row_len - 1)
    acc = (column(1) + b_ref[...]
           + jnp.where(pos != 0, pltpu.roll(column(0), 1, 0), 0.0)
           + jnp.where(pos != row_len - 1, pltpu.roll(column(2), tb - 1, 0), 0.0))
    o_ref[...] = _silu(acc)


def _conv(p, col0_blocks, w9, bias, n_lat_rows, seq, tc):
    n_rows = p.shape[0]
    ch = w9.shape[1]
    tb = SCAN_BLOCK
    halo_per_blk = tb // GRID_W
    n_halo_rows = n_rows // GRID_W
    kern = functools.partial(_conv_kernel, n_lat_blocks=n_lat_rows // tb, blocks_per_seq=seq // tb)
    return pl.pallas_call(
        kern,
        grid=(n_rows // tb, ch // tc),
        in_specs=[pl.BlockSpec((tb, tc), lambda i, j: (i, col0_blocks + j)),
                  pl.BlockSpec((GRID_W, tc), lambda i, j: (jnp.maximum(i * halo_per_blk - 1, 0), col0_blocks + j)),
                  pl.BlockSpec((GRID_W, tc),
                               lambda i, j: (jnp.minimum((i + 1) * halo_per_blk, n_halo_rows - 1), col0_blocks + j)),
                  pl.BlockSpec((9, tc), lambda i, j: (0, j)),
                  pl.BlockSpec((1, tc), lambda i, j: (0, j))],
        out_specs=pl.BlockSpec((tb, tc), lambda i, j: (i, j)),
        out_shape=jax.ShapeDtypeStruct((n_rows, ch), F32),
        compiler_params=_cparams(("arbitrary", "arbitrary")),
        name="short_conv",
    )(p, p, p, w9, bias)


def _cumsum_op(tb, length, reverse):
    r = lax.broadcasted_iota(jnp.int32, (tb, tb), 0)
    c = lax.broadcasted_iota(jnp.int32, (tb, tb), 1)
    shift = length.bit_length() - 1
    same = (r >> shift) == (c >> shift)
    return jnp.logical_and(same, (c >= r) if reverse else (c <= r)).astype(BF16)


def _apply_op(op, x):
    hi = x.astype(BF16)
    r1 = x - hi.astype(F32)
    mid = r1.astype(BF16)
    lo = (r1 - mid.astype(F32)).astype(BF16)
    n = x.shape[1]
    out = _dot(op, jnp.concatenate([hi, mid, lo], axis=1))
    return out[:, 0:n] + out[:, n:2 * n] + out[:, 2 * n:3 * n]


def _causal(length, reverse):
    r = lax.broadcasted_iota(jnp.int32, (length, length), 0)
    c = lax.broadcasted_iota(jnp.int32, (length, length), 1)
    return (c >= r) if reverse else (c <= r)


def _tail_refs(refs, has_prev, n_scratch):
    tail = refs[len(refs) - n_scratch - 1 - int(has_prev):]
    return tail if has_prev else (None,) + tuple(tail)


def _plus_prev(val, prev_ref, rows, cols):
    return val if prev_ref is None else val + prev_ref[rows, cols]


def _chunk_order(tb, reverse):
    idx = list(range(tb // CHUNK))
    return idx[::-1] if reverse else idx


LANE = 128
LOG2_E = 1.4426950408889634


def _half_masks():
    lane_lo = lax.broadcasted_iota(jnp.int32, (1, LANE), 1) < LANE // 2
    sub_lo = lax.broadcasted_iota(jnp.int32, (LANE, 1), 0) < LANE // 2
    return lane_lo, sub_lo


def _ssd_block(xs_ref, bm_ref, cm_ref, sm_ref, bias, aneg, prev_ref, y_ref, row0, s_ref, op, lmask, *, d, reverse, heads,
               hdim):
    tb = xs_ref.shape[0]
    hpg = heads // SSD_GROUPS
    assert hdim == LANE // 2 and SSD_STATE == LANE // 2 and hpg % 2 == 0 and SSD_GROUPS % 2 == 0
    end = 0 if reverse else tb - 1
    lane_lo, sub_lo = _half_masks()
    dt = _softplus(sm_ref[...] + bias)
    acum2 = _apply_op(op, dt * aneg) * LOG2_E
    acum2_t = acum2.T
    src_t = (acum2 - jnp.log2(dt)).T
    w_t = jnp.exp2(acum2_t[:, end:end + 1] - src_t)
    e_end = jnp.exp2(acum2[end:end + 1, :])
    xs = xs_ref[...].astype(BF16)
    bm_t = bm_ref[...].T
    cm = cm_ref[...].astype(BF16)
    for gp in range(SSD_GROUPS // 2):
        cm_pair = cm[:, gp * LANE:(gp + 1) * LANE]
        bm_pair_t = bm_t[gp * LANE:(gp + 1) * LANE, :]
        for gi in range(2):
            g = 2 * gp + gi
            in_g = sub_lo if gi == 0 else jnp.logical_not(sub_lo)
            cb = _dot(cm_pair, jnp.where(in_g, bm_pair_t, 0.0).astype(BF16))
            bmg_t = bm_t[g * SSD_STATE:(g + 1) * SSD_STATE, :]
            s_g = s_ref[g]
            zero = jnp.zeros_like(s_g)
            s_rows = jnp.concatenate([s_g, zero] if gi == 0 else [zero, s_g], axis=0).astype(BF16)
            y_in = _dot(cm_pair, s_rows)
            for pi in range(hpg // 2):
                gl = slice(pi * LANE, (pi + 1) * LANE)
                xl = slice(g * hpg * hdim + pi * LANE, g * hpg * hdim + (pi + 1) * LANE)
                xs_pair = xs[:, xl]
                xs_half = (jnp.where(lane_lo, xs_pair, 0.0).astype(BF16), jnp.where(lane_lo, 0.0, xs_pair).astype(BF16))
                c0 = d * heads + g * hpg + 2 * pi
                y_acc = None
                s_acc = None
                e_col = []
                for hi in range(2):
                    c = c0 + hi
                    colb = jnp.broadcast_to(acum2[:, c:c + 1], (tb, tb))
                    m = cb * jnp.exp2(jnp.where(lmask, colb - src_t[c:c + 1, :], -jnp.inf))
                    y_h = _dot(m.astype(BF16), xs_half[hi])
                    s_h = _dot((bmg_t * w_t[c:c + 1, :]).astype(BF16), xs_half[hi])
                    y_acc = y_h if y_acc is None else y_acc + y_h
                    s_acc = s_h if s_acc is None else s_acc + s_h
                    e_col.append(jnp.exp2(colb[:, 0:LANE]))
                y_ref[row0:row0 + tb, xl] = _plus_prev(y_acc + y_in[:, gl] * jnp.where(lane_lo, e_col[0], e_col[1]),
                                                       prev_ref, slice(row0, row0 + tb), xl)
                e_pair = jnp.where(lane_lo, e_end[:, c0:c0 + 1], e_end[:, c0 + 1:c0 + 2])
                s_ref[g, :, gl] = s_g[:, gl] * e_pair + s_acc
                yield


def _ssd_kernel(*refs, batch, d, reverse, heads, hdim, has_prev):
    ins, (bias_ref, aneg_ref), (prev_ref, y_ref, s_scr) = refs[:4 * batch], refs[4 * batch:4 * batch + 2], _tail_refs(refs, has_prev, 1)

    @pl.when(pl.program_id(0) == 0)
    def _():
        s_scr[...] = jnp.zeros_like(s_scr)

    tb = SCAN_BLOCK
    op = _cumsum_op(tb, tb, reverse)
    lmask = _causal(tb, reverse)
    _interleave(_ssd_block(*ins[4 * b:4 * b + 4], bias_ref[...], aneg_ref[...], prev_ref, y_ref, b * tb, s_scr.at[b], op, lmask,
                           d=d, reverse=reverse, heads=heads, hdim=hdim) for b in range(batch))


def _ml_kernel(*refs, batch, d, reverse, heads, gate_col0, has_prev):
    ins, (gbi_ref, gbf_ref), (prev_ref, y_ref, ct_scr, m_scr) = refs[:5 * batch], refs[5 * batch:5 * batch + 2], _tail_refs(refs, has_prev, 2)

    @pl.when(pl.program_id(0) == 0)
    def _():
        ct_scr[...] = jnp.zeros_like(ct_scr)
        m_scr[...] = jnp.zeros_like(m_scr)

    tb = SCAN_BLOCK
    op = _cumsum_op(tb, tb, reverse)
    lmask = _causal(tb, reverse)
    states = [([ct_scr[b, h] for h in range(heads)], [m_scr[b, h][:, 0:1] for h in range(heads)]) for b in range(batch)]
    new = [([], []) for _ in range(batch)]
    _interleave(_ml_block(*ins[5 * b:5 * b + 5], gbi_ref[...], gbf_ref[...], prev_ref, y_ref, b * tb, states[b][0], states[b][1],
                          new[b][0], new[b][1], op, lmask, d=d, reverse=reverse, heads=heads, gate_col0=gate_col0)
                for b in range(batch))
    for b in range(batch):
        for h in range(heads):
            ct_scr[b, h] = new[b][0][h]
            m_scr[b, h] = jnp.broadcast_to(new[b][1][h], m_scr.shape[2:])


def _cummax_rows(x, reverse):
    n = x.shape[0]
    row = lax.broadcasted_iota(jnp.int32, (n, 1), 0)
    s = 1
    while s < n:
        if s < 8:
            if reverse:
                shifted = jnp.where(row < n - s, pltpu.roll(x, n - s, 0), -jnp.inf)
            else:
                shifted = jnp.where(row >= s, pltpu.roll(x, s, 0), -jnp.inf)
        else:
            pad = jnp.full((s, x.shape[1]), -jnp.inf, F32)
            shifted = jnp.concatenate([x[s:], pad] if reverse else [pad, x[:n - s]], axis=0)
        x = jnp.maximum(x, shifted)
        s *= 2
    return x


def _ml_block(q_ref, k_ref, v_ref, smi_ref, smf_ref, gate_bi, gate_bf, prev_ref, y_ref, row0, ct_in, m_in, ct_out, m_out, op, lmask,
              *, d, reverse, heads, gate_col0):
    tb = q_ref.shape[0]
    hd = q_ref.shape[1] // heads
    assert hd == LANE
    end = 0 if reverse else tb - 1
    base = gate_col0 + d * 2 * heads
    raw = smi_ref[...] + gate_bi
    bcum = _apply_op(op, _log_sigmoid(smf_ref[...] + gate_bf))
    r = raw - bcum
    r_t = r.T
    cmax = _cummax_rows(r, reverse)
    q_all = (q_ref[...] * (hd ** -0.5)).astype(BF16)
    k_all = k_ref[...].astype(BF16)
    v_all = v_ref[...].astype(BF16)
    ones = jnp.ones((tb, hd), BF16)
    yield
    for h in range(heads):
        c = base + h
        hs = slice(h * hd, (h + 1) * hd)
        qb = q_all[:, hs]
        k_t = k_all[:, hs].T
        v_ext = jnp.concatenate([v_all[:, hs], ones], axis=1)
        m_st = m_in[h]
        ct = ct_in[h]
        r_row = r_t[c:c + 1, :]
        b_rep = jnp.broadcast_to(bcum[:, c:c + 1], (tb, hd))
        u_rep = jnp.broadcast_to(jnp.maximum(cmax[:, c:c + 1], m_st), (tb, hd))
        s = jnp.exp(jnp.where(lmask, r_row - jnp.concatenate([u_rep, u_rep], axis=1), -jnp.inf)) * _dot(qb, k_t)
        g_rep = jnp.exp(m_st - u_rep)
        acc = _dot(s.astype(BF16), v_ext) + jnp.concatenate([g_rep, g_rep], axis=1) * _dot(qb, ct.astype(BF16))
        y_ref[row0:row0 + tb, hs] = _plus_prev(acc[:, 0:hd] / jnp.maximum(jnp.abs(acc[:, hd:]), jnp.exp(-(b_rep + u_rep))),
                                               prev_ref, slice(row0, row0 + tb), hs)
        b_end = bcum[end:end + 1, c:c + 1]
        a_row = b_end + r_row
        m_loc = jnp.max(a_row, axis=1, keepdims=True)
        upd = _dot((k_t * jnp.exp(a_row - m_loc)).astype(BF16), v_ext)
        m_new = jnp.maximum(b_end + m_st, m_loc)
        ct_out.append(jnp.exp(b_end + m_st - m_new) * ct + jnp.exp(m_loc - m_new) * upd)
        m_out.append(m_new)
        yield


def _gla_block(q_ref, k_ref, v_ref, sm_ref, w2, b2, prev_ref, y_ref, row0, s_scr, op, lmask, *, reverse, heads):
    tb = q_ref.shape[0]
    dk = q_ref.shape[1] // heads
    dv = v_ref.shape[1] // heads
    assert dk == LANE // 2 and heads % 2 == 0
    lane_lo, _ = _half_masks()
    x = _dot(sm_ref[...].astype(BF16), w2) + b2
    b = _apply_op(op, _log_sigmoid(x) * (1.0 / GLA_TAU))
    q_d = (q_ref[...] * (dk ** -0.5) * jnp.exp(b)).astype(BF16)
    k = k_ref[...]
    k_d = k * jnp.exp(-b)
    v = v_ref[...].astype(BF16)
    state = [s_scr[h] for h in range(heads)]
    for ci in _chunk_order(tb, reverse):
        lo = ci * CHUNK
        sl = slice(lo, lo + CHUNK)
        end = lo if reverse else lo + CHUNK - 1
        b_end = b[end:end + 1, :]
        kk = k[sl] * jnp.exp(b_end - b[sl])
        e_end = jnp.exp(b_end)
        att, inter = [], []
        for h in range(heads):
            pl_ = slice(h // 2 * LANE, (h // 2 + 1) * LANE)
            mine = lane_lo if h % 2 == 0 else jnp.logical_not(lane_lo)
            vs = slice(h * dv, (h + 1) * dv)
            q_pair = q_d[sl, pl_]
            k_m = jnp.where(mine, k_d[sl, pl_], 0.0).astype(BF16)
            kk_m = jnp.where(mine, kk[:, pl_], 0.0).astype(BF16)
            att.append(_dot_nt(q_pair, k_m))
            inter.append(_dot_nt(q_pair, state[h].astype(BF16)))
            state[h] = state[h] * e_end[:, pl_] + _dot_tn(v[sl, vs], kk_m)
        yield
        outs = [_dot(jnp.where(lmask, att[h], 0.0).astype(BF16), v[sl, h * dv:(h + 1) * dv]) + inter[h]
                for h in range(heads)]
        rows = slice(row0 + lo, row0 + lo + CHUNK)
        y_ref[rows, :] = _plus_prev(jnp.concatenate(outs, axis=1), prev_ref, rows, slice(None))
        yield
    for h in range(heads):
        s_scr[h] = state[h]


def _interleave(gens):
    gens = list(gens)
    while gens:
        alive = []
        for g in gens:
            try:
                next(g)
                alive.append(g)
            except StopIteration:
                pass
        gens = alive


def _gla_kernel(*refs, batch, reverse, heads, has_prev):
    ins, (w2_ref, b2_ref), (prev_ref, y_ref, s_scr) = refs[:4 * batch], refs[4 * batch:4 * batch + 2], _tail_refs(refs, has_prev, 1)

    @pl.when(pl.program_id(0) == 0)
    def _():
        s_scr[...] = jnp.zeros_like(s_scr)

    tb = SCAN_BLOCK
    op = _cumsum_op(tb, CHUNK, reverse)
    lmask = _causal(CHUNK, reverse)
    _interleave(_gla_block(*ins[4 * b:4 * b + 4], w2_ref[...], b2_ref[...], prev_ref, y_ref, b * tb, s_scr.at[b], op, lmask,
                           reverse=reverse, heads=heads) for b in range(batch))


def _scan_rows(batch, seq, ctx_len, reverse):
    tb = SCAN_BLOCK
    nc, nl = ctx_len // tb, seq // tb
    ctx_base = batch * seq // tb

    def rows(b, s):
        if reverse:
            return jnp.where(s < nc, ctx_base + b * nc + (nc - 1 - s), b * nl + (nl - 1 - (s - nc)))
        return jnp.where(s < nc, ctx_base + b * nc + s, b * nl + (s - nc))

    def fwd_step(s):
        return jnp.where(s < nc, nc - 1 - s, nc + nl - 1 - (s - nc)) if reverse else s

    return rows, fwd_step, nc + nl


def _scan_out_rows(batch, seq, ctx_len, reverse, tile):
    tb = SCAN_BLOCK
    nc, nl = ctx_len // tb, seq // tb
    sub = tb // tile

    def rows(i):
        q, part = i // sub, i % sub
        is_lat = q < batch * nl
        qc = jnp.maximum(q - batch * nl, 0)
        b = jnp.where(is_lat, q // nl, qc // nc)
        k = jnp.where(is_lat, q % nl, qc % nc)
        n = jnp.where(is_lat, nl, nc)
        s = jnp.where(is_lat, nc, 0) + (n - 1 - k if reverse else k)
        return (s * batch + b) * sub + part

    return rows


def _scan_call(kern, name, batch, rows, steps, ins, consts, out_w, scratch, out_step=None, prev=None):
    tb = SCAN_BLOCK
    out_map = (lambda s: (s, 0)) if out_step is None else (lambda s: (out_step(s), 0))
    in_specs = [pl.BlockSpec((tb, w), functools.partial(lambda s, b, cb: (rows(b, s), cb), b=b, cb=cb))
                for b in range(batch) for _, w, cb in ins]
    in_specs += [pl.BlockSpec(c.shape, lambda s: (0, 0)) for c in consts]
    args = [a for _ in range(batch) for a, _, _ in ins] + list(consts)
    if prev is not None:
        in_specs.append(pl.BlockSpec((batch * tb, out_w), out_map))
        args.append(prev)
    return pl.pallas_call(
        functools.partial(kern, has_prev=prev is not None),
        grid=(steps,),
        in_specs=in_specs,
        out_specs=pl.BlockSpec((batch * tb, out_w), out_map),
        out_shape=jax.ShapeDtypeStruct((steps * batch * tb, out_w), F32),
        scratch_shapes=scratch,
        compiler_params=_cparams(("arbitrary",)),
        name=name,
    )(*args)


def _group_norm(y, groups, center):
    gw = y.shape[1] // groups
    outs = []
    for g in range(groups):
        yg = y[:, g * gw:(g + 1) * gw]
        if center:
            yg = yg - jnp.mean(yg, axis=-1, keepdims=True)
        outs.append(yg * lax.rsqrt(jnp.mean(yg * yg, axis=-1, keepdims=True) + EPS))
    return jnp.concatenate(outs, axis=1)


def _merge_kernel(h_ref, mod_ref, gate_ref, z_ref, xs_ref, ys_ref, o_ref, ym_ref, r_ref, yg_ref,
                  dskip_ref, snw_ref, mnw_ref, gnw_ref, mb_ref,
                  wbs_ref, wbm_ref, wbg_ref, wo_ref, lng_ref, lnb_ref, out_ref, *, alpha, ml_heads):
    d = h_ref.shape[1]
    ys = (ys_ref[...] + dskip_ref[...] * xs_ref[...]) * _silu(z_ref[...])
    ys = _group_norm(ys, SSD_GROUPS, False) * snw_ref[...]
    b_ssd = _dot(ys.astype(BF16), wbs_ref[...])
    ym = _sigmoid(o_ref[...]) * ym_ref[...]
    ym = _group_norm(ym, ml_heads, True) * mnw_ref[...]
    b_ml = _dot(ym.astype(BF16), wbm_ref[...])
    yg = _group_norm(yg_ref[...], GLA_HEADS, True) * gnw_ref[...] * _silu(r_ref[...])
    b_gla = _dot(yg.astype(BF16), wbg_ref[...])
    mb = mb_ref[...]
    mix = (_sigmoid(gate_ref[:, 0:d] + mb[0:1]) * b_ssd
           + _sigmoid(gate_ref[:, d:2 * d] + mb[1:2]) * b_ml
           + _sigmoid(gate_ref[:, 2 * d:3 * d] + mb[2:3]) * b_gla)
    y = _dot(mix.astype(BF16), wo_ref[...])
    m = mod_ref[0]
    out_ref[...] = _layer_norm(alpha * h_ref[...] + m[2:3] * y, lng_ref[...], lnb_ref[...])


def kernel(x, c, ctx, c_ctx, w_mod, b_mod, ln_g, ln_b, ffn_w_in, ffn_w_out, w_in, merge_b, ssd_conv_w, ssd_conv_b, ssd_dt_bias, ssd_a_log, ssd_d, ssd_norm_w, ml_conv_w, ml_conv_b, ml_gate_b, ml_norm_w, gla_w2, gla_b2, gla_norm_w, w_br_ssd, w_br_ml, w_br_gla, w_out):
    batch, seq, d = x.shape
    ctx_len = ctx.shape[1]
    depth = w_mod.shape[0]
    alpha = (2 * depth) ** 0.25
    ssd_inner = ssd_norm_w.shape[1]
    ssd_heads = ssd_dt_bias.shape[2]
    ssd_hdim = ssd_inner // ssd_heads
    bc_w = 2 * SSD_GROUPS * SSD_STATE
    ml_w = ml_norm_w.shape[1]
    ml_heads = ml_gate_b.shape[3]
    gla_w = gla_norm_w.shape[1]
    gla_kw = gla_w2.shape[3]
    gla_rank = gla_w2.shape[2]
    n_lat = batch * seq
    n_all = n_lat + batch * ctx_len
    tb = SCAN_BLOCK
    assert seq % tb == 0 and ctx_len == tb
    assert 2 * ssd_heads + 4 * ml_heads + 2 * gla_rank <= SMALL_W

    tm = next(t for t in (1024, 512, tb) if seq % t == 0 and (batch * ctx_len) % t == 0)
    tm_merge = 256
    tf = 512

    def group_of_tile_for(t):
        per_batch = seq // t
        return lambda i: jnp.minimum(i // per_batch, batch)

    assert (d, ssd_inner, ml_w, gla_w, gla_kw, bc_w) == (2048, 1024, 512, 512, 256, 512)
    gate_w = 3 * d
    col_z = gate_w
    col_mlv = col_z + ssd_inner
    col_mlqk = col_mlv + ml_w
    col_xs = col_mlqk + 2 * ml_w
    col_bc = col_xs + ssd_inner
    col_mlo = col_bc + bc_w
    col_glav = col_mlo + ml_w
    col_glar = col_glav + gla_w
    col_glaq = col_glar + gla_w
    col_glak = col_glaq + gla_kw
    col_small = col_glak + gla_kw
    tn_proj = 1280
    n_proj = -(-(col_small + 2 * SMALL_W) // tn_proj) * tn_proj
    conv_w = 2 * ml_w + ssd_inner + bc_w
    tc_conv = conv_w // 2
    assert col_mlqk % tc_conv == 0

    in_sizes = (ssd_inner, ssd_inner + bc_w, 2 * ssd_heads, ml_w, ml_w, ml_w, ml_w, 4 * ml_heads,
                gla_kw, gla_kw, gla_w, gla_w, 2 * gla_rank, gate_w)
    offs = [0]
    for sz in in_sizes:
        offs.append(offs[-1] + sz)
    (o_z, o_xbc, o_dt, o_mlq, o_mlk, o_mlv, o_mlo, o_mlg, o_gq, o_gk, o_gv, o_gr, o_lr, o_gate, o_end) = offs

    rows_mod = 8
    cpad = jnp.concatenate([c, c_ctx[None], jnp.zeros((rows_mod - batch - 1, d), F32)], axis=0)
    mod = _modulation(cpad, w_mod, b_mod).reshape(depth, rows_mod, N_MOD, d)

    ffn_in_bf = ffn_w_in.astype(BF16)
    ffn_out_bf = ffn_w_out.astype(BF16)
    w_br_bf = [w.astype(BF16) for w in (w_br_ssd, w_br_ml, w_br_gla)]
    w_out_bf = w_out.astype(BF16)
    g0 = 2 * ssd_heads
    small_used = 2 * ssd_heads + 4 * ml_heads + 2 * gla_rank
    fgate = []
    for dd in range(2):
        f0 = o_mlg + (2 * dd + 1) * ml_heads
        fgate += [(f0, f0 + ml_heads), ml_heads]
    w_t = jnp.swapaxes(w_in, 1, 2).astype(BF16)
    small_segs = ([(o_dt, o_mlq), (o_mlg, o_gq), (o_lr, o_gate), SMALL_W - small_used, g0] + fgate
                  + [n_proj - col_small - SMALL_W - g0 - 4 * ml_heads])
    tail = jnp.concatenate([jnp.zeros((depth, seg, d), BF16) if isinstance(seg, int) else w_t[:, seg[0]:seg[1]]
                            for seg in small_segs], axis=1)
    wide_segs = [(o_gate, o_end), (o_z, o_xbc), (o_mlv, o_mlo), (o_mlq, o_mlv), (o_xbc, o_dt), (o_mlo, o_mlg),
                 (o_gv, o_lr), (o_gq, o_gv)]
    w_big = _gather_rows(w_t, tail, wide_segs)
    assert w_big.shape[1] == n_proj

    h = None
    for l in range(depth):
        last = l == depth - 1
        n_out = n_lat if last else n_all

        conv_w9 = jnp.concatenate([ml_conv_w[l].reshape(9, 2 * ml_w), ssd_conv_w[l].reshape(9, ssd_inner + bc_w)], axis=1)
        conv_b = jnp.concatenate([ml_conv_b[l], ssd_conv_b[l]])[None]

        def small_row(vals, start):
            return jnp.zeros((1, SMALL_W), F32).at[0, start:start + vals.shape[0]].set(vals)

        dt_bias_row = small_row(ssd_dt_bias[l].reshape(-1), 0)
        a_neg_row = small_row(-jnp.exp(ssd_a_log[l].astype(F32)).reshape(-1), 0)
        gate_bi_row = small_row(ml_gate_b[l].reshape(-1), g0)
        gate_bf_row = small_row(jnp.concatenate([ml_gate_b[l, :, 1], jnp.zeros((2, ml_heads), F32)], axis=1).reshape(-1), g0)
        lr0 = 2 * ssd_heads + 4 * ml_heads
        w2_pad = [jnp.zeros((SMALL_W, gla_kw), F32).at[lr0 + dd * gla_rank:lr0 + (dd + 1) * gla_rank].set(gla_w2[l, dd])
                  for dd in range(2)]

        ffn0 = functools.partial(_ffn, mod3=mod[l, :, 0:3], group_of_tile=group_of_tile_for(tm), w_in_bf=ffn_in_bf,
                                 w_out_bf=ffn_out_bf, lk=(l, 0), ln_g=ln_g[l, 0], ln_b=ln_b[l, 0], alpha=alpha,
                                 tm=tm, tf=tf)
        if h is None:
            h = ffn0(x.reshape(n_lat, d), n_lat, out_rows=n_all)
            h = ffn0(ctx.reshape(batch * ctx_len, d), batch * ctx_len, out_rows=n_all, tile0=n_lat // tm, out_buf=h)
        else:
            h = ffn0(h, n_all)

        p = _proj(h, mod[l, :, 3:6], group_of_tile_for(tm), w_big, l, tm, tn_proj)
        cv = _conv(p, col_mlqk // tc_conv, conv_w9, conv_b, n_lat, seq, tc_conv)
        col_cvb = 2 * ml_w + ssd_inner
        ys = ym = yg = None
        for dd, reverse in ((0, False), (1, True)):
            rows, fwd_step, steps = _scan_rows(batch, seq, ctx_len, reverse)
            ys = _scan_call(
                functools.partial(_ssd_kernel, batch=batch, d=dd, reverse=reverse, heads=ssd_heads, hdim=ssd_hdim),
                "ssd_scan", batch, rows, steps,
                [(cv, ssd_inner, 2 * ml_w // ssd_inner), (cv, bc_w // 2, col_cvb // (bc_w // 2)),
                 (cv, bc_w // 2, col_cvb // (bc_w // 2) + 1), (p, SMALL_W, col_small // SMALL_W)],
                [dt_bias_row, a_neg_row], ssd_inner,
                [pltpu.VMEM((batch, SSD_GROUPS, SSD_STATE, ssd_inner // SSD_GROUPS), F32)], fwd_step, ys)
            ym = _scan_call(
                functools.partial(_ml_kernel, batch=batch, d=dd, reverse=reverse, heads=ml_heads,
                                  gate_col0=2 * ssd_heads),
                "mlstm_scan", batch, rows, steps,
                [(cv, ml_w, 0), (cv, ml_w, 1), (p, ml_w, col_mlv // ml_w), (p, SMALL_W, col_small // SMALL_W),
                 (p, SMALL_W, col_small // SMALL_W + 1)],
                [gate_bi_row, gate_bf_row], ml_w,
                [pltpu.VMEM((batch, ml_heads, ml_w // ml_heads, 2 * ml_w // ml_heads), F32),
                 pltpu.VMEM((batch, ml_heads, 1, ml_w // ml_heads), F32)], fwd_step, ym)
            yg = _scan_call(
                functools.partial(_gla_kernel, batch=batch, reverse=reverse, heads=GLA_HEADS),
                "gla_scan", batch, rows, steps,
                [(p, gla_kw, col_glaq // gla_kw), (p, gla_kw, col_glak // gla_kw), (p, gla_w, col_glav // gla_w),
                 (p, SMALL_W, col_small // SMALL_W)],
                [w2_pad[dd].astype(BF16), gla_b2[l, dd][None]], gla_w,
                [pltpu.VMEM((batch, GLA_HEADS, gla_w // GLA_HEADS, LANE), F32)], fwd_step, yg)

        tmm = tm_merge
        row = lambda i: (i, 0)
        const = lambda i: (0, 0)
        y_rows = _scan_out_rows(batch, seq, ctx_len, False, tmm)
        row_y = lambda i: (y_rows(i), 0)

        def col(cb):
            return lambda i: (i, cb)

        g_of = group_of_tile_for(tmm)
        consts = [jnp.repeat(ssd_d[l], ssd_hdim)[None], ssd_norm_w[l][None], ml_norm_w[l][None], gla_norm_w[l][None],
                  merge_b[l]]
        weights = [w_br_bf[0], w_br_bf[1], w_br_bf[2], w_out_bf]
        ln_consts = [ln_g[l, 1][None], ln_b[l, 1][None]]
        h_mix = pl.pallas_call(
            functools.partial(_merge_kernel, alpha=alpha, ml_heads=ml_heads),
            grid=(n_out // tmm,),
            in_specs=[pl.BlockSpec((tmm, d), row),
                      pl.BlockSpec((1, 3, d), lambda i: (g_of(i), 0, 0)),
                      pl.BlockSpec((tmm, gate_w), col(0)),
                      pl.BlockSpec((tmm, ssd_inner), col(col_z // ssd_inner)),
                      pl.BlockSpec((tmm, ssd_inner), col(2 * ml_w // ssd_inner)),
                      pl.BlockSpec((tmm, ssd_inner), row_y),
                      pl.BlockSpec((tmm, ml_w), col(col_mlo // ml_w)),
                      pl.BlockSpec((tmm, ml_w), row_y),
                      pl.BlockSpec((tmm, gla_w), col(col_glar // gla_w)),
                      pl.BlockSpec((tmm, gla_w), row_y)]
                     + [pl.BlockSpec(a.shape, const) for a in consts]
                     + [pl.BlockSpec((None,) + a.shape[1:], functools.partial(lambda i, l: (l, 0, 0), l=l),
                                     pipeline_mode=pl.Buffered(1))
                        for a in weights]
                     + [pl.BlockSpec(a.shape, const) for a in ln_consts],
            out_specs=pl.BlockSpec((tmm, d), row),
            out_shape=jax.ShapeDtypeStruct((n_out, d), F32),
            compiler_params=_cparams(("arbitrary",)),
            name="merge_out",
        )(h, mod[l, :, 3:6], p, p, cv, ys, p, ym, p, yg, *consts, *weights, *ln_consts)

        h = _ffn(h_mix, n_out, mod[l, :, 6:9], group_of_tile_for(tm), ffn_in_bf, ffn_out_bf, (l, 1), ln_g[l, 2],
                 ln_b[l, 2], alpha, tm, tf)

    return h.reshape(batch, seq, d)
```

```python
import functools

import jax
import jax.numpy as jnp
from jax import lax
from jax.experimental import pallas as pl
from jax.experimental.pallas import tpu as pltpu

F32 = jnp.float32
BF16 = jnp.bfloat16

GRID_W = 64
CHUNK = 64
SSD_GROUPS = 4
SSD_STATE = 64
GLA_HEADS = 4
GLA_TAU = 16.0
EPS = 1e-5
N_MOD = 9

SCAN_BLOCK = 256
SMALL_W = 128
V7X_VMEM_LIMIT = 56 * 1024 * 1024
V7X_VMEM_LIMIT_BIG = 60 * 1024 * 1024

NT_DIMS = (((1,), (1,)), ((), ()))
TN_DIMS = (((0,), (0,)), ((), ()))


def _cparams(sem, vmem=V7X_VMEM_LIMIT):
    return pltpu.CompilerParams(dimension_semantics=sem, vmem_limit_bytes=vmem)


def _dot(a, b):
    return jnp.dot(a, b, preferred_element_type=F32)


def _dot_nt(a, b):
    return lax.dot_general(a, b, NT_DIMS, preferred_element_type=F32)


def _dot_tn(a, b):
    return lax.dot_general(a, b, TN_DIMS, preferred_element_type=F32)


def _sigmoid(x):
    return 1.0 / (1.0 + jnp.exp(-x))


def _silu(x):
    return x * _sigmoid(x)


def _softplus(x):
    return jnp.maximum(x, 0.0) + jnp.log(1.0 + jnp.exp(-jnp.abs(x)))


def _log_sigmoid(x):
    return -_softplus(-x)


def _layer_norm(x, g, b):
    mu = jnp.mean(x, axis=-1, keepdims=True)
    xc = x - mu
    var = jnp.mean(xc * xc, axis=-1, keepdims=True)
    return xc * lax.rsqrt(var + EPS) * g + b


def _mod_kernel(c_ref, w_ref, b_ref, o_ref):
    c = c_ref[...]
    o_ref[0] = _dot(_silu(c).astype(BF16), w_ref[0].astype(BF16)) + b_ref[0]


def _modulation(cpad, w_mod, b_mod, tn=1152):
    depth, d, n = w_mod.shape
    rows = cpad.shape[0]
    return pl.pallas_call(
        _mod_kernel,
        grid=(depth, n // tn),
        in_specs=[pl.BlockSpec((rows, d), lambda l, j: (0, 0)),
                  pl.BlockSpec((1, d, tn), lambda l, j: (l, 0, j)),
                  pl.BlockSpec((1, 1, tn), lambda l, j: (l, 0, j))],
        out_specs=pl.BlockSpec((1, rows, tn), lambda l, j: (l, 0, j)),
        out_shape=jax.ShapeDtypeStruct((depth, rows, n), F32),
        compiler_params=_cparams(("arbitrary", "arbitrary")),
        name="modulation",
    )(cpad, w_mod, b_mod.reshape(depth, 1, n))


def _ffn_kernel(h_ref, mod_ref, wa_ref, wg_ref, wo_ref, lng_ref, lnb_ref, o_ref, *, alpha):
    j = pl.program_id(1)
    m = mod_ref[0]
    u = (h_ref[...] * (1.0 + m[1:2]) + m[0:1]).astype(BF16)
    a = _dot(u, wa_ref[...])
    g = _dot(u, wg_ref[...])

    @pl.when(j == 0)
    def _():
        o_ref[...] = jnp.zeros_like(o_ref)

    o_ref[...] += _dot((_silu(g) * a).astype(BF16), wo_ref[...])

    @pl.when(j == pl.num_programs(1) - 1)
    def _():
        x = alpha * h_ref[...] + m[2:3] * (0.5 * o_ref[...])
        o_ref[...] = _layer_norm(x, lng_ref[...], lnb_ref[...])


def _ffn_kernel_inplace(h_ref, mod_ref, wa_ref, wg_ref, wo_ref, lng_ref, lnb_ref, buf_ref, o_ref, *, alpha):
    del buf_ref
    _ffn_kernel(h_ref, mod_ref, wa_ref, wg_ref, wo_ref, lng_ref, lnb_ref, o_ref, alpha=alpha)


def _ffn(h, n_rows, mod3, group_of_tile, w_in_bf, w_out_bf, lk, ln_g, ln_b, alpha, tm, tf, out_rows=None, tile0=0,
         out_buf=None):
    d = h.shape[1]
    dff = w_out_bf.shape[2]
    nf = dff // tf
    l, k = lk
    out_rows = n_rows if out_rows is None else out_rows
    in_specs = [pl.BlockSpec((tm, d), lambda i, j: (i, 0)),
                pl.BlockSpec((1, 3, d), lambda i, j: (group_of_tile(i + tile0), 0, 0)),
                pl.BlockSpec((None, None, d, tf), lambda i, j: (l, k, 0, j)),
                pl.BlockSpec((None, None, d, tf), lambda i, j: (l, k, 0, j + nf)),
                pl.BlockSpec((None, None, tf, d), lambda i, j: (l, k, j, 0)),
                pl.BlockSpec((1, d), lambda i, j: (0, 0)),
                pl.BlockSpec((1, d), lambda i, j: (0, 0))]
    args = [h, mod3, w_in_bf, w_in_bf, w_out_bf, ln_g.reshape(1, d), ln_b.reshape(1, d)]
    kern = functools.partial(_ffn_kernel, alpha=alpha)
    aliases = {}
    if out_buf is not None:
        in_specs.append(pl.BlockSpec(memory_space=pl.ANY))
        args.append(out_buf)
        aliases = {len(args) - 1: 0}
        kern = functools.partial(_ffn_kernel_inplace, alpha=alpha)
    return pl.pallas_call(
        kern,
        grid=(n_rows // tm, nf),
        in_specs=in_specs,
        out_specs=pl.BlockSpec((tm, d), lambda i, j: (i + tile0, 0)),
        out_shape=jax.ShapeDtypeStruct((out_rows, d), F32),
        input_output_aliases=aliases,
        compiler_params=_cparams(("arbitrary", "arbitrary"), V7X_VMEM_LIMIT_BIG),
        name="ffn",
    )(*args)


ROW_ALIGN = 16


def _gather_rows_kernel(starts_ref, src_ref, tail_ref, dst_ref, *, n_src_blocks):
    del starts_ref
    j = pl.program_id(1)

    @pl.when(j < n_src_blocks)
    def _():
        dst_ref[...] = src_ref[0]

    @pl.when(j >= n_src_blocks)
    def _():
        dst_ref[...] = tail_ref[...]


def _gather_rows(src, tail, segments, tr=512):
    depth, _, d = src.shape
    assert all((b - a) % tr == 0 and a % ROW_ALIGN == 0 for a, b in segments) and tail.shape[1] % tr == 0
    starts = [a + k for a, b in segments for k in range(0, b - a, tr)]
    n_src, n_tail = len(starts), tail.shape[1] // tr
    starts = jnp.asarray(starts + [0] * n_tail, jnp.int32)
    grid_spec = pltpu.PrefetchScalarGridSpec(
        num_scalar_prefetch=1,
        grid=(depth, n_src + n_tail),
        in_specs=[pl.BlockSpec((pl.Element(1), pl.Element(tr), pl.Element(d)),
                               lambda l, j, st: (l, pl.multiple_of(st[j], ROW_ALIGN), 0)),
                  pl.BlockSpec((None, tr, d), lambda l, j, st: (l, jnp.maximum(j - n_src, 0), 0))],
        out_specs=pl.BlockSpec((None, tr, d), lambda l, j, st: (l, j, 0)))
    return pl.pallas_call(
        functools.partial(_gather_rows_kernel, n_src_blocks=n_src),
        grid_spec=grid_spec,
        out_shape=jax.ShapeDtypeStruct((depth, (n_src + n_tail) * tr, d), src.dtype),
        compiler_params=_cparams(("arbitrary", "arbitrary")),
        name="gather_rows",
    )(starts, src, tail)


def _proj_kernel(h_ref, mod_ref, w_ref, o_ref):
    m = mod_ref[0]
    u = (h_ref[...] * (1.0 + m[1:2]) + m[0:1]).astype(BF16)
    o_ref[...] = _dot_nt(u, w_ref[...])


def _proj(h, mod3, group_of_tile, w_big, l, tm, tn):
    n_rows, d = h.shape
    n = w_big.shape[1]
    return pl.pallas_call(
        _proj_kernel,
        grid=(n_rows // tm, n // tn),
        in_specs=[pl.BlockSpec((tm, d), lambda i, j: (i, 0)),
                  pl.BlockSpec((1, 3, d), lambda i, j: (group_of_tile(i), 0, 0)),
                  pl.BlockSpec((None, tn, d), lambda i, j: (l, j, 0))],
        out_specs=pl.BlockSpec((tm, tn), lambda i, j: (i, j)),
        out_shape=jax.ShapeDtypeStruct((n_rows, n), F32),
        compiler_params=_cparams(("arbitrary", "arbitrary")),
        name="in_proj",
    )(h, mod3, w_big)


def _conv_kernel(c_ref, p_ref, n_ref, w_ref, b_ref, o_ref, *, n_lat_blocks, blocks_per_seq, ctx_len):
    i = pl.program_id(0)
    tb = c_ref.shape[0]
    is_lat = i < n_lat_blocks
    in_seq = i % blocks_per_seq
    keep_prev = jnp.logical_and(is_lat, in_seq != 0)
    keep_next = jnp.logical_and(is_lat, in_seq != blocks_per_seq - 1)
    cur = c_ref[...]
    up = jnp.concatenate([jnp.where(keep_prev, p_ref[...], 0.0), cur[:tb - GRID_W]], axis=0)
    down = jnp.concatenate([cur[GRID_W:], jnp.where(keep_next, n_ref[...], 0.0)], axis=0)
    w = w_ref[...]
    vert = jnp.where(is_lat, 1.0, 0.0)
    w_up, w_mid, w_dn = w[0:3] * vert, w[3:6], w[6:9] * vert

    def column(dw):
        return w_up[dw:dw + 1] * up + w_mid[dw:dw + 1] * cur + w_dn[dw:dw + 1] * down

    r = lax.broadcasted_iota(jnp.int32, (tb, 1), 0)
    row_len = jnp.where(is_lat, GRID_W, ctx_len)
    pos = r & (row_len - 1)
    acc = (column(1) + b_ref[...]
           + jnp.where(pos != 0, pltpu.roll(column(0), 1, 0), 0.0)
           + jnp.where(pos != row_len - 1, pltpu.roll(column(2), tb - 1, 0), 0.0))
    o_ref[...] = _silu(acc)


def _conv(p, col0_blocks, w9, bias, n_lat_rows, seq, ctx_len, tb, tc):
    n_rows = p.shape[0]
    ch = w9.shape[1]
    assert seq % tb == 0 and n_rows % tb == 0 and tb % ctx_len == 0 and ctx_len & (ctx_len - 1) == 0
    halo_per_blk = tb // GRID_W
    n_halo_rows = n_rows // GRID_W
    kern = functools.partial(_conv_kernel, n_lat_blocks=n_lat_rows // tb, blocks_per_seq=seq // tb, ctx_len=ctx_len)
    return pl.pallas_call(
        kern,
        grid=(n_rows // tb, ch // tc),
        in_specs=[pl.BlockSpec((tb, tc), lambda i, j: (i, col0_blocks + j)),
                  pl.BlockSpec((GRID_W, tc), lambda i, j: (jnp.maximum(i * halo_per_blk - 1, 0), col0_blocks + j)),
                  pl.BlockSpec((GRID_W, tc),
                               lambda i, j: (jnp.minimum((i + 1) * halo_per_blk, n_halo_rows - 1), col0_blocks + j)),
                  pl.BlockSpec((9, tc), lambda i, j: (0, j)),
                  pl.BlockSpec((1, tc), lambda i, j: (0, j))],
        out_specs=pl.BlockSpec((tb, tc), lambda i, j: (i, j)),
        out_shape=jax.ShapeDtypeStruct((n_rows, ch), F32),
        compiler_params=_cparams(("arbitrary", "arbitrary")),
        name="short_conv",
    )(p, p, p, w9, bias)


def _cumsum_op(tb, length, reverse):
    r = lax.broadcasted_iota(jnp.int32, (tb, tb), 0)
    c = lax.broadcasted_iota(jnp.int32, (tb, tb), 1)
    shift = length.bit_length() - 1
    same = (r >> shift) == (c >> shift)
    return jnp.logical_and(same, (c >= r) if reverse else (c <= r)).astype(BF16)


def _apply_op(op, x):
    hi = x.astype(BF16)
    r1 = x - hi.astype(F32)
    mid = r1.astype(BF16)
    lo = (r1 - mid.astype(F32)).astype(BF16)
    n = x.shape[1]
    out = _dot(op, jnp.concatenate([hi, mid, lo], axis=1))
    return out[:, 0:n] + out[:, n:2 * n] + out[:, 2 * n:3 * n]


def _causal(length, reverse):
    r = lax.broadcasted_iota(jnp.int32, (length, length), 0)
    c = lax.broadcasted_iota(jnp.int32, (length, length), 1)
    return (c >= r) if reverse else (c <= r)


def _tail_refs(refs, has_prev, n_scratch):
    tail = refs[len(refs) - n_scratch - 1 - int(has_prev):]
    return tail if has_prev else (None,) + tuple(tail)


def _plus_prev(val, prev_ref, rows, cols):
    return val if prev_ref is None else val + prev_ref[rows, cols]


def _chunk_order(tb, reverse):
    idx = list(range(tb // CHUNK))
    return idx[::-1] if reverse else idx


LANE = 128
LOG2_E = 1.4426950408889634


def _half_masks():
    lane_lo = lax.broadcasted_iota(jnp.int32, (1, LANE), 1) < LANE // 2
    sub_lo = lax.broadcasted_iota(jnp.int32, (LANE, 1), 0) < LANE // 2
    return lane_lo, sub_lo


def _ssd_block(xs_ref, bm_ref, cm_ref, sm_ref, bias, aneg, prev_ref, y_ref, row0, s_ref, op, lmask, *, d, reverse, heads,
               hdim):
    tb = xs_ref.shape[0]
    hpg = heads // SSD_GROUPS
    assert hdim == LANE // 2 and SSD_STATE == LANE // 2 and hpg % 2 == 0 and SSD_GROUPS % 2 == 0
    end = 0 if reverse else tb - 1
    lane_lo, sub_lo = _half_masks()
    dt = _softplus(sm_ref[...] + bias)
    acum2 = _apply_op(op, dt * aneg) * LOG2_E
    acum2_t = acum2.T
    src_t = (acum2 - jnp.log2(dt)).T
    w_t = jnp.exp2(acum2_t[:, end:end + 1] - src_t)
    e_end = jnp.exp2(acum2[end:end + 1, :])
    xs = xs_ref[...].astype(BF16)
    bm_t = bm_ref[...].T
    cm = cm_ref[...].astype(BF16)
    for gp in range(SSD_GROUPS // 2):
        cm_pair = cm[:, gp * LANE:(gp + 1) * LANE]
        bm_pair_t = bm_t[gp * LANE:(gp + 1) * LANE, :]
        for gi in range(2):
            g = 2 * gp + gi
            in_g = sub_lo if gi == 0 else jnp.logical_not(sub_lo)
            cb = _dot(cm_pair, jnp.where(in_g, bm_pair_t, 0.0).astype(BF16))
            bmg_t = bm_t[g * SSD_STATE:(g + 1) * SSD_STATE, :]
            s_g = s_ref[g]
            zero = jnp.zeros_like(s_g)
            s_rows = jnp.concatenate([s_g, zero] if gi == 0 else [zero, s_g], axis=0).astype(BF16)
            y_in = _dot(cm_pair, s_rows)
            for pi in range(hpg // 2):
                gl = slice(pi * LANE, (pi + 1) * LANE)
                xl = slice(g * hpg * hdim + pi * LANE, g * hpg * hdim + (pi + 1) * LANE)
                xs_pair = xs[:, xl]
                xs_half = (jnp.where(lane_lo, xs_pair, 0.0).astype(BF16), jnp.where(lane_lo, 0.0, xs_pair).astype(BF16))
                c0 = d * heads + g * hpg + 2 * pi
                y_acc = None
                s_acc = None
                e_col = []
                for hi in range(2):
                    c = c0 + hi
                    colb = jnp.broadcast_to(acum2[:, c:c + 1], (tb, tb))
                    m = cb * jnp.exp2(jnp.where(lmask, colb - src_t[c:c + 1, :], -jnp.inf))
                    y_h = _dot(m.astype(BF16), xs_half[hi])
                    s_h = _dot((bmg_t * w_t[c:c + 1, :]).astype(BF16), xs_half[hi])
                    y_acc = y_h if y_acc is None else y_acc + y_h
                    s_acc = s_h if s_acc is None else s_acc + s_h
                    e_col.append(jnp.exp2(colb[:, 0:LANE]))
                y_ref[row0:row0 + tb, xl] = _plus_prev(y_acc + y_in[:, gl] * jnp.where(lane_lo, e_col[0], e_col[1]),
                                                       prev_ref, slice(row0, row0 + tb), xl)
                e_pair = jnp.where(lane_lo, e_end[:, c0:c0 + 1], e_end[:, c0 + 1:c0 + 2])
                s_ref[g, :, gl] = s_g[:, gl] * e_pair + s_acc
                yield


def _ssd_kernel(*refs, batch, d, reverse, heads, hdim, has_prev):
    ins, (bias_ref, aneg_ref), (prev_ref, y_ref, s_scr) = refs[:4 * batch], refs[4 * batch:4 * batch + 2], _tail_refs(refs, has_prev, 1)

    @pl.when(pl.program_id(0) == 0)
    def _():
        s_scr[...] = jnp.zeros_like(s_scr)

    tb = SCAN_BLOCK
    op = _cumsum_op(tb, tb, reverse)
    lmask = _causal(tb, reverse)
    _interleave(_ssd_block(*ins[4 * b:4 * b + 4], bias_ref[...], aneg_ref[...], prev_ref, y_ref, b * tb, s_scr.at[b], op, lmask,
                           d=d, reverse=reverse, heads=heads, hdim=hdim) for b in range(batch))


def _ml_kernel(*refs, batch, d, reverse, heads, gate_col0, has_prev):
    ins, (gbi_ref, gbf_ref), (prev_ref, y_ref, ct_scr, m_scr) = refs[:5 * batch], refs[5 * batch:5 * batch + 2], _tail_refs(refs, has_prev, 2)

    @pl.when(pl.program_id(0) == 0)
    def _():
        ct_scr[...] = jnp.zeros_like(ct_scr)
        m_scr[...] = jnp.zeros_like(m_scr)

    tb = SCAN_BLOCK
    op = _cumsum_op(tb, tb, reverse)
    lmask = _causal(tb, reverse)
    states = [([ct_scr[b, h] for h in range(heads)], [m_scr[b, h][:, 0:1] for h in range(heads)]) for b in range(batch)]
    new = [([], []) for _ in range(batch)]
    _interleave(_ml_block(*ins[5 * b:5 * b + 5], gbi_ref[...], gbf_ref[...], prev_ref, y_ref, b * tb, states[b][0], states[b][1],
                          new[b][0], new[b][1], op, lmask, d=d, reverse=reverse, heads=heads, gate_col0=gate_col0)
                for b in range(batch))
    for b in range(batch):
        for h in range(heads):
            ct_scr[b, h] = new[b][0][h]
            m_scr[b, h] = jnp.broadcast_to(new[b][1][h], m_scr.shape[2:])


def _cummax_rows(x, reverse):
    n = x.shape[0]
    row = lax.broadcasted_iota(jnp.int32, (n, 1), 0)
    s = 1
    while s < n:
        if s < 8:
            if reverse:
                shifted = jnp.where(row < n - s, pltpu.roll(x, n - s, 0), -jnp.inf)
            else:
                shifted = jnp.where(row >= s, pltpu.roll(x, s, 0), -jnp.inf)
        else:
            pad = jnp.full((s, x.shape[1]), -jnp.inf, F32)
            shifted = jnp.concatenate([x[s:], pad] if reverse else [pad, x[:n - s]], axis=0)
        x = jnp.maximum(x, shifted)
        s *= 2
    return x


def _ml_block(q_ref, k_ref, v_ref, smi_ref, smf_ref, gate_bi, gate_bf, prev_ref, y_ref, row0, ct_in, m_in, ct_out, m_out, op, lmask,
              *, d, reverse, heads, gate_col0):
    tb = q_ref.shape[0]
    hd = q_ref.shape[1] // heads
    assert hd == LANE
    end = 0 if reverse else tb - 1
    base = gate_col0 + d * 2 * heads
    raw = smi_ref[...] + gate_bi
    bcum = _apply_op(op, _log_sigmoid(smf_ref[...] + gate_bf))
    r = raw - bcum
    r_t = r.T
    cmax = _cummax_rows(r, reverse)
    q_all = (q_ref[...] * (hd ** -0.5)).astype(BF16)
    k_all = k_ref[...].astype(BF16)
    v_all = v_ref[...].astype(BF16)
    ones = jnp.ones((tb, hd), BF16)
    yield
    for h in range(heads):
        c = base + h
        hs = slice(h * hd, (h + 1) * hd)
        qb = q_all[:, hs]
        k_t = k_all[:, hs].T
        v_ext = jnp.concatenate([v_all[:, hs], ones], axis=1)
        m_st = m_in[h]
        ct = ct_in[h]
        r_row = r_t[c:c + 1, :]
        b_rep = jnp.broadcast_to(bcum[:, c:c + 1], (tb, hd))
        u_rep = jnp.broadcast_to(jnp.maximum(cmax[:, c:c + 1], m_st), (tb, hd))
        s = jnp.exp(jnp.where(lmask, r_row - jnp.concatenate([u_rep, u_rep], axis=1), -jnp.inf)) * _dot(qb, k_t)
        g_rep = jnp.exp(m_st - u_rep)
        acc = _dot(s.astype(BF16), v_ext) + jnp.concatenate([g_rep, g_rep], axis=1) * _dot(qb, ct.astype(BF16))
        y_ref[row0:row0 + tb, hs] = _plus_prev(acc[:, 0:hd] / jnp.maximum(jnp.abs(acc[:, hd:]), jnp.exp(-(b_rep + u_rep))),
                                               prev_ref, slice(row0, row0 + tb), hs)
        b_end = bcum[end:end + 1, c:c + 1]
        a_row = b_end + r_row
        m_loc = jnp.max(a_row, axis=1, keepdims=True)
        upd = _dot((k_t * jnp.exp(a_row - m_loc)).astype(BF16), v_ext)
        m_new = jnp.maximum(b_end + m_st, m_loc)
        ct_out.append(jnp.exp(b_end + m_st - m_new) * ct + jnp.exp(m_loc - m_new) * upd)
        m_out.append(m_new)
        yield


def _gla_block(q_ref, k_ref, v_ref, sm_ref, w2, b2, prev_ref, y_ref, row0, s_scr, op, lmask, *, reverse, heads):
    tb = q_ref.shape[0]
    dk = q_ref.shape[1] // heads
    dv = v_ref.shape[1] // heads
    assert dk == LANE // 2 and heads % 2 == 0
    lane_lo, _ = _half_masks()
    x = _dot(sm_ref[...].astype(BF16), w2) + b2
    b = _apply_op(op, _log_sigmoid(x) * (1.0 / GLA_TAU))
    q_d = (q_ref[...] * (dk ** -0.5) * jnp.exp(b)).astype(BF16)
    k = k_ref[...]
    k_d = k * jnp.exp(-b)
    v = v_ref[...].astype(BF16)
    state = [s_scr[h] for h in range(heads)]
    for ci in _chunk_order(tb, reverse):
        lo = ci * CHUNK
        sl = slice(lo, lo + CHUNK)
        end = lo if reverse else lo + CHUNK - 1
        b_end = b[end:end + 1, :]
        kk = k[sl] * jnp.exp(b_end - b[sl])
        e_end = jnp.exp(b_end)
        att, inter = [], []
        for h in range(heads):
            pl_ = slice(h // 2 * LANE, (h // 2 + 1) * LANE)
            mine = lane_lo if h % 2 == 0 else jnp.logical_not(lane_lo)
            vs = slice(h * dv, (h + 1) * dv)
            q_pair = q_d[sl, pl_]
            k_m = jnp.where(mine, k_d[sl, pl_], 0.0).astype(BF16)
            kk_m = jnp.where(mine, kk[:, pl_], 0.0).astype(BF16)
            att.append(_dot_nt(q_pair, k_m))
            inter.append(_dot_nt(q_pair, state[h].astype(BF16)))
            state[h] = state[h] * e_end[:, pl_] + _dot_tn(v[sl, vs], kk_m)
        yield
        outs = [_dot(jnp.where(lmask, att[h], 0.0).astype(BF16), v[sl, h * dv:(h + 1) * dv]) + inter[h]
                for h in range(heads)]
        rows = slice(row0 + lo, row0 + lo + CHUNK)
        y_ref[rows, :] = _plus_prev(jnp.concatenate(outs, axis=1), prev_ref, rows, slice(None))
        yield
    for h in range(heads):
        s_scr[h] = state[h]


def _interleave(gens):
    gens = list(gens)
    while gens:
        alive = []
        for g in gens:
            try:
                next(g)
                alive.append(g)
            except StopIteration:
                pass
        gens = alive


def _gla_kernel(*refs, batch, reverse, heads, has_prev):
    ins, (w2_ref, b2_ref), (prev_ref, y_ref, s_scr) = refs[:4 * batch], refs[4 * batch:4 * batch + 2], _tail_refs(refs, has_prev, 1)

    @pl.when(pl.program_id(0) == 0)
    def _():
        s_scr[...] = jnp.zeros_like(s_scr)

    tb = SCAN_BLOCK
    op = _cumsum_op(tb, CHUNK, reverse)
    lmask = _causal(CHUNK, reverse)
    _interleave(_gla_block(*ins[4 * b:4 * b + 4], w2_ref[...], b2_ref[...], prev_ref, y_ref, b * tb, s_scr.at[b], op, lmask,
                           reverse=reverse, heads=heads) for b in range(batch))


def _scan_rows(batch, seq, ctx_len, reverse):
    tb = SCAN_BLOCK
    nc, nl = ctx_len // tb, seq // tb
    ctx_base = batch * seq // tb

    def rows(b, s):
        if reverse:
            return jnp.where(s < nc, ctx_base + b * nc + (nc - 1 - s), b * nl + (nl - 1 - (s - nc)))
        return jnp.where(s < nc, ctx_base + b * nc + s, b * nl + (s - nc))

    def fwd_step(s):
        return jnp.where(s < nc, nc - 1 - s, nc + nl - 1 - (s - nc)) if reverse else s

    return rows, fwd_step, nc + nl


def _scan_out_rows(batch, seq, ctx_len, reverse, tile):
    tb = SCAN_BLOCK
    nc, nl = ctx_len // tb, seq // tb
    sub = tb // tile

    def rows(i):
        q, part = i // sub, i % sub
        is_lat = q < batch * nl
        qc = jnp.maximum(q - batch * nl, 0)
        b = jnp.where(is_lat, q // nl, qc // nc)
        k = jnp.where(is_lat, q % nl, qc % nc)
        n = jnp.where(is_lat, nl, nc)
        s = jnp.where(is_lat, nc, 0) + (n - 1 - k if reverse else k)
        return (s * batch + b) * sub + part

    return rows


def _scan_call(kern, name, batch, rows, steps, ins, consts, out_w, scratch, out_step=None, prev=None):
    tb = SCAN_BLOCK
    out_map = (lambda s: (s, 0)) if out_step is None else (lambda s: (out_step(s), 0))
    in_specs = [pl.BlockSpec((tb, w), functools.partial(lambda s, b, cb: (rows(b, s), cb), b=b, cb=cb))
                for b in range(batch) for _, w, cb in ins]
    in_specs += [pl.BlockSpec(c.shape, lambda s: (0, 0)) for c in consts]
    args = [a for _ in range(batch) for a, _, _ in ins] + list(consts)
    if prev is not None:
        in_specs.append(pl.BlockSpec((batch * tb, out_w), out_map))
        args.append(prev)
    return pl.pallas_call(
        functools.partial(kern, has_prev=prev is not None),
        grid=(steps,),
        in_specs=in_specs,
        out_specs=pl.BlockSpec((batch * tb, out_w), out_map),
        out_shape=jax.ShapeDtypeStruct((steps * batch * tb, out_w), F32),
        scratch_shapes=scratch,
        compiler_params=_cparams(("arbitrary",)),
        name=name,
    )(*args)


def _group_norm(y, groups, center):
    gw = y.shape[1] // groups
    outs = []
    for g in range(groups):
        yg = y[:, g * gw:(g + 1) * gw]
        if center:
            yg = yg - jnp.mean(yg, axis=-1, keepdims=True)
        outs.append(yg * lax.rsqrt(jnp.mean(yg * yg, axis=-1, keepdims=True) + EPS))
    return jnp.concatenate(outs, axis=1)


def _merge_kernel(h_ref, mod_ref, gate_ref, z_ref, xs_ref, ys_ref, o_ref, ym_ref, r_ref, yg_ref,
                  dskip_ref, snw_ref, mnw_ref, gnw_ref, mb_ref,
                  wbs_ref, wbm_ref, wbg_ref, wo_ref, lng_ref, lnb_ref, out_ref, *, alpha, ml_heads):
    d = h_ref.shape[1]
    ys = (ys_ref[...] + dskip_ref[...] * xs_ref[...]) * _silu(z_ref[...])
    ys = _group_norm(ys, SSD_GROUPS, False) * snw_ref[...]
    b_ssd = _dot(ys.astype(BF16), wbs_ref[...])
    ym = _sigmoid(o_ref[...]) * ym_ref[...]
    ym = _group_norm(ym, ml_heads, True) * mnw_ref[...]
    b_ml = _dot(ym.astype(BF16), wbm_ref[...])
    yg = _group_norm(yg_ref[...], GLA_HEADS, True) * gnw_ref[...] * _silu(r_ref[...])
    b_gla = _dot(yg.astype(BF16), wbg_ref[...])
    mb = mb_ref[...]
    mix = (_sigmoid(gate_ref[:, 0:d] + mb[0:1]) * b_ssd
           + _sigmoid(gate_ref[:, d:2 * d] + mb[1:2]) * b_ml
           + _sigmoid(gate_ref[:, 2 * d:3 * d] + mb[2:3]) * b_gla)
    y = _dot(mix.astype(BF16), wo_ref[...])
    m = mod_ref[0]
    out_ref[...] = _layer_norm(alpha * h_ref[...] + m[2:3] * y, lng_ref[...], lnb_ref[...])


def kernel(x, c, ctx, c_ctx, w_mod, b_mod, ln_g, ln_b, ffn_w_in, ffn_w_out, w_in, merge_b, ssd_conv_w, ssd_conv_b, ssd_dt_bias, ssd_a_log, ssd_d, ssd_norm_w, ml_conv_w, ml_conv_b, ml_gate_b, ml_norm_w, gla_w2, gla_b2, gla_norm_w, w_br_ssd, w_br_ml, w_br_gla, w_out):
    batch, seq, d = x.shape
    ctx_len = ctx.shape[1]
    depth = w_mod.shape[0]
    alpha = (2 * depth) ** 0.25
    ssd_inner = ssd_norm_w.shape[1]
    ssd_heads = ssd_dt_bias.shape[2]
    ssd_hdim = ssd_inner // ssd_heads
    bc_w = 2 * SSD_GROUPS * SSD_STATE
    ml_w = ml_norm_w.shape[1]
    ml_heads = ml_gate_b.shape[3]
    gla_w = gla_norm_w.shape[1]
    gla_kw = gla_w2.shape[3]
    gla_rank = gla_w2.shape[2]
    n_lat = batch * seq
    n_all = n_lat + batch * ctx_len
    tb = SCAN_BLOCK
    assert seq % tb == 0 and ctx_len == tb
    assert 2 * ssd_heads + 4 * ml_heads + 2 * gla_rank <= SMALL_W

    tm = next(t for t in (1024, 512, tb) if seq % t == 0 and (batch * ctx_len) % t == 0)
    tm_merge = 256
    tf = 512
    tb_conv = next(t for t in (512, tb) if seq % t == 0 and (batch * ctx_len) % t == 0)

    def group_of_tile_for(t):
        per_batch = seq // t
        return lambda i: jnp.minimum(i // per_batch, batch)

    assert (d, ssd_inner, ml_w, gla_w, gla_kw, bc_w) == (2048, 1024, 512, 512, 256, 512)
    gate_w = 3 * d
    col_z = gate_w
    col_mlv = col_z + ssd_inner
    col_mlqk = col_mlv + ml_w
    col_xs = col_mlqk + 2 * ml_w
    col_bc = col_xs + ssd_inner
    col_mlo = col_bc + bc_w
    col_glav = col_mlo + ml_w
    col_glar = col_glav + gla_w
    col_glaq = col_glar + gla_w
    col_glak = col_glaq + gla_kw
    col_small = col_glak + gla_kw
    tn_proj = 1280
    n_proj = -(-(col_small + 2 * SMALL_W) // tn_proj) * tn_proj
    conv_w = 2 * ml_w + ssd_inner + bc_w
    tc_conv = conv_w // 2
    assert col_mlqk % tc_conv == 0

    in_sizes = (ssd_inner, ssd_inner + bc_w, 2 * ssd_heads, ml_w, ml_w, ml_w, ml_w, 4 * ml_heads,
                gla_kw, gla_kw, gla_w, gla_w, 2 * gla_rank, gate_w)
    offs = [0]
    for sz in in_sizes:
        offs.append(offs[-1] + sz)
    (o_z, o_xbc, o_dt, o_mlq, o_mlk, o_mlv, o_mlo, o_mlg, o_gq, o_gk, o_gv, o_gr, o_lr, o_gate, o_end) = offs

    rows_mod = 8
    cpad = jnp.concatenate([c, c_ctx[None], jnp.zeros((rows_mod - batch - 1, d), F32)], axis=0)
    mod = _modulation(cpad, w_mod, b_mod).reshape(depth, rows_mod, N_MOD, d)

    ffn_in_bf = ffn_w_in.astype(BF16)
    ffn_out_bf = ffn_w_out.astype(BF16)
    w_br_bf = [w.astype(BF16) for w in (w_br_ssd, w_br_ml, w_br_gla)]
    w_out_bf = w_out.astype(BF16)
    g0 = 2 * ssd_heads
    small_used = 2 * ssd_heads + 4 * ml_heads + 2 * gla_rank
    fgate = []
    for dd in range(2):
        f0 = o_mlg + (2 * dd + 1) * ml_heads
        fgate += [(f0, f0 + ml_heads), ml_heads]
    w_t = jnp.swapaxes(w_in, 1, 2).astype(BF16)
    small_segs = ([(o_dt, o_mlq), (o_mlg, o_gq), (o_lr, o_gate), SMALL_W - small_used, g0] + fgate
                  + [n_proj - col_small - SMALL_W - g0 - 4 * ml_heads])
    tail = jnp.concatenate([jnp.zeros((depth, seg, d), BF16) if isinstance(seg, int) else w_t[:, seg[0]:seg[1]]
                            for seg in small_segs], axis=1)
    wide_segs = [(o_gate, o_end), (o_z, o_xbc), (o_mlv, o_mlo), (o_mlq, o_mlv), (o_xbc, o_dt), (o_mlo, o_mlg),
                 (o_gv, o_lr), (o_gq, o_gv)]
    w_big = _gather_rows(w_t, tail, wide_segs)
    assert w_big.shape[1] == n_proj

    h = None
    for l in range(depth):
        last = l == depth - 1
        n_out = n_lat if last else n_all

        conv_w9 = jnp.concatenate([ml_conv_w[l].reshape(9, 2 * ml_w), ssd_conv_w[l].reshape(9, ssd_inner + bc_w)], axis=1)
        conv_b = jnp.concatenate([ml_conv_b[l], ssd_conv_b[l]])[None]

        def small_row(vals, start):
            return jnp.zeros((1, SMALL_W), F32).at[0, start:start + vals.shape[0]].set(vals)

        dt_bias_row = small_row(ssd_dt_bias[l].reshape(-1), 0)
        a_neg_row = small_row(-jnp.exp(ssd_a_log[l].astype(F32)).reshape(-1), 0)
        gate_bi_row = small_row(ml_gate_b[l].reshape(-1), g0)
        gate_bf_row = small_row(jnp.concatenate([ml_gate_b[l, :, 1], jnp.zeros((2, ml_heads), F32)], axis=1).reshape(-1), g0)
        lr0 = 2 * ssd_heads + 4 * ml_heads
        w2_pad = [jnp.zeros((SMALL_W, gla_kw), F32).at[lr0 + dd * gla_rank:lr0 + (dd + 1) * gla_rank].set(gla_w2[l, dd])
                  for dd in range(2)]

        ffn0 = functools.partial(_ffn, mod3=mod[l, :, 0:3], group_of_tile=group_of_tile_for(tm), w_in_bf=ffn_in_bf,
                                 w_out_bf=ffn_out_bf, lk=(l, 0), ln_g=ln_g[l, 0], ln_b=ln_b[l, 0], alpha=alpha,
                                 tm=tm, tf=tf)
        if h is None:
            h = ffn0(x.reshape(n_lat, d), n_lat, out_rows=n_all)
            h = ffn0(ctx.reshape(batch * ctx_len, d), batch * ctx_len, out_rows=n_all, tile0=n_lat // tm, out_buf=h)
        else:
            h = ffn0(h, n_all)

        p = _proj(h, mod[l, :, 3:6], group_of_tile_for(tm), w_big, l, tm, tn_proj)
        cv = _conv(p, col_mlqk // tc_conv, conv_w9, conv_b, n_lat, seq, ctx_len, tb_conv, tc_conv)
        col_cvb = 2 * ml_w + ssd_inner
        ys = ym = yg = None
        for dd, reverse in ((0, False), (1, True)):
            rows, fwd_step, steps = _scan_rows(batch, seq, ctx_len, reverse)
            ys = _scan_call(
                functools.partial(_ssd_kernel, batch=batch, d=dd, reverse=reverse, heads=ssd_heads, hdim=ssd_hdim),
                "ssd_scan", batch, rows, steps,
                [(cv, ssd_inner, 2 * ml_w // ssd_inner), (cv, bc_w // 2, col_cvb // (bc_w // 2)),
                 (cv, bc_w // 2, col_cvb // (bc_w // 2) + 1), (p, SMALL_W, col_small // SMALL_W)],
                [dt_bias_row, a_neg_row], ssd_inner,
                [pltpu.VMEM((batch, SSD_GROUPS, SSD_STATE, ssd_inner // SSD_GROUPS), F32)], fwd_step, ys)
            ym = _scan_call(
                functools.partial(_ml_kernel, batch=batch, d=dd, reverse=reverse, heads=ml_heads,
                                  gate_col0=2 * ssd_heads),
                "mlstm_scan", batch, rows, steps,
                [(cv, ml_w, 0), (cv, ml_w, 1), (p, ml_w, col_mlv // ml_w), (p, SMALL_W, col_small // SMALL_W),
                 (p, SMALL_W, col_small // SMALL_W + 1)],
                [gate_bi_row, gate_bf_row], ml_w,
                [pltpu.VMEM((batch, ml_heads, ml_w // ml_heads, 2 * ml_w // ml_heads), F32),
                 pltpu.VMEM((batch, ml_heads, 1, ml_w // ml_heads), F32)], fwd_step, ym)
            yg = _scan_call(
                functools.partial(_gla_kernel, batch=batch, reverse=reverse, heads=GLA_HEADS),
                "gla_scan", batch, rows, steps,
                [(p, gla_kw, col_glaq // gla_kw), (p, gla_kw, col_glak // gla_kw), (p, gla_w, col_glav // gla_w),
                 (p, SMALL_W, col_small // SMALL_W)],
                [w2_pad[dd].astype(BF16), gla_b2[l, dd][None]], gla_w,
                [pltpu.VMEM((batch, GLA_HEADS, gla_w // GLA_HEADS, LANE), F32)], fwd_step, yg)

        tmm = tm_merge
        row = lambda i: (i, 0)
        const = lambda i: (0, 0)
        y_rows = _scan_out_rows(batch, seq, ctx_len, False, tmm)
        row_y = lambda i: (y_rows(i), 0)

        def col(cb):
            return lambda i: (i, cb)

        g_of = group_of_tile_for(tmm)
        consts = [jnp.repeat(ssd_d[l], ssd_hdim)[None], ssd_norm_w[l][None], ml_norm_w[l][None], gla_norm_w[l][None],
                  merge_b[l]]
        weights = [w_br_bf[0], w_br_bf[1], w_br_bf[2], w_out_bf]
        ln_consts = [ln_g[l, 1][None], ln_b[l, 1][None]]
        h_mix = pl.pallas_call(
            functools.partial(_merge_kernel, alpha=alpha, ml_heads=ml_heads),
            grid=(n_out // tmm,),
            in_specs=[pl.BlockSpec((tmm, d), row),
                      pl.BlockSpec((1, 3, d), lambda i: (g_of(i), 0, 0)),
                      pl.BlockSpec((tmm, gate_w), col(0)),
                      pl.BlockSpec((tmm, ssd_inner), col(col_z // ssd_inner)),
                      pl.BlockSpec((tmm, ssd_inner), col(2 * ml_w // ssd_inner)),
                      pl.BlockSpec((tmm, ssd_inner), row_y),
                      pl.BlockSpec((tmm, ml_w), col(col_mlo // ml_w)),
                      pl.BlockSpec((tmm, ml_w), row_y),
                      pl.BlockSpec((tmm, gla_w), col(col_glar // gla_w)),
                      pl.BlockSpec((tmm, gla_w), row_y)]
                     + [pl.BlockSpec(a.shape, const) for a in consts]
                     + [pl.BlockSpec((None,) + a.shape[1:], functools.partial(lambda i, l: (l, 0, 0), l=l),
                                     pipeline_mode=pl.Buffered(1))
                        for a in weights]
                     + [pl.BlockSpec(a.shape, const) for a in ln_consts],
            out_specs=pl.BlockSpec((tmm, d), row),
            out_shape=jax.ShapeDtypeStruct((n_out, d), F32),
            compiler_params=_cparams(("arbitrary",)),
            name="merge_out",
        )(h, mod[l, :, 3:6], p, p, cv, ys, p, ym, p, yg, *consts, *weights, *ln_consts)

        h = _ffn(h_mix, n_out, mod[l, :, 6:9], group_of_tile_for(tm), ffn_in_bf, ffn_out_bf, (l, 1), ln_g[l, 2],
                 ln_b[l, 2], alpha, tm, tf)

    return h.reshape(batch, seq, d)
```
